```python
import math
import jax, jax.numpy as jnp
from jax import lax
import numpy as np

D_MODEL = 1024
BATCH = 2
SEQ = 8192
DEPTH = 2
DEC_BATCH = 32
DEC_SEQ = 1
PAST_LEN = 16384
PAGE_SIZE = 128

D_RNN = 1024
N_RNN_BLOCKS = 8
RNN_BLOCK = D_RNN // N_RNN_BLOCKS
CONV_WIDTH = 4
LRU_C = 8.0
N_DH = 4
HEAD_DIM = 64
V_DIM = 2 * HEAD_DIM
D_ATT = N_DH * 2 * HEAD_DIM
D_ATT_V = N_DH * V_DIM
Q_BLOCK = 128
D_FF = 4 * D_MODEL
EPS = 1e-6

kernel_name = "hawk_diffattn_hybrid_decode_step"


def rms_norm(x, g):
    xf = x.astype(jnp.float32)
    y = xf * lax.rsqrt(jnp.mean(xf * xf, axis=-1, keepdims=True) + EPS)
    return (y * g.astype(jnp.float32)).astype(x.dtype)


def ada_mod(c, w, b):
    m = jax.nn.silu(c) @ w + b
    return [t[:, None, :] for t in jnp.split(m, 6, axis=-1)]


def causal_conv(xp, w, b):
    L = xp.shape[1] - (CONV_WIDTH - 1)
    return sum(w[j] * xp[:, j:j + L] for j in range(CONV_WIDTH)) + b


def rg_lru(x, h0, w_a, b_a, w_x, b_x, lam):
    xb = x.reshape(x.shape[0], x.shape[1], N_RNN_BLOCKS, RNN_BLOCK)
    ga = jnp.einsum('blnc,ncd->blnd', xb, w_a).reshape(x.shape) + b_a
    gx = jnp.einsum('blnc,ncd->blnd', xb, w_x).reshape(x.shape) + b_x
    r = jax.nn.sigmoid(ga.astype(jnp.float32))
    i = jax.nn.sigmoid(gx.astype(jnp.float32))
    log_a = -LRU_C * r * jax.nn.softplus(-lam.astype(jnp.float32))
    a = jnp.exp(log_a)
    bterm = jnp.sqrt(-jnp.expm1(2.0 * log_a)) * (i * x.astype(jnp.float32))

    def combine(p, q):
        a1, b1 = p
        a2, b2 = q
        return a1 * a2, a2 * b1 + b2

    A, Bc = lax.associative_scan(combine, (a, bterm), axis=1)
    h = A * h0.astype(jnp.float32)[:, None, :] + Bc
    return h.astype(x.dtype), h[:, -1]


def diff_attn_prompt(q, k, v, lam):
    B, S = q.shape[0], q.shape[1]
    scale = HEAD_DIM ** -0.5
    kpos = jnp.arange(S)

    def block(i):
        qb = lax.dynamic_slice_in_dim(q, i * Q_BLOCK, Q_BLOCK, axis=1)
        s = jnp.einsum('bqhmd,bkhmd->bhmqk', qb, k).astype(jnp.float32) * scale
        qpos = i * Q_BLOCK + jnp.arange(Q_BLOCK)
        s = jnp.where(kpos[None, :] <= qpos[:, None], s, -jnp.inf)
        p = jax.nn.softmax(s, axis=-1)
        w = p[:, :, 0] - lam * p[:, :, 1]
        return jnp.einsum('bhqk,bkhe->bqhe', w.astype(v.dtype), v)

    out = lax.map(block, jnp.arange(S // Q_BLOCK))
    return out.transpose(1, 0, 2, 3, 4).reshape(B, S, N_DH, V_DIM)


def diff_attn_sample(q, k_new, v_new, k_past, v_past, lam):
    k = jnp.concatenate([k_past.astype(k_new.dtype), k_new], axis=1)
    v = jnp.concatenate([v_past.astype(v_new.dtype), v_new], axis=1)
    T, Lq = k.shape[1], q.shape[1]
    past = T - Lq
    s = jnp.einsum('bqhmd,bkhmd->bhmqk', q, k).astype(jnp.float32) * (HEAD_DIM ** -0.5)
    mask = jnp.arange(T)[None, :] <= past + jnp.arange(Lq)[:, None]
    s = jnp.where(mask, s, -jnp.inf)
    p = jax.nn.softmax(s, axis=-1)
    w = p[:, :, 0] - lam * p[:, :, 1]
    return jnp.einsum('bhqk,bkhe->bqhe', w.astype(v.dtype), v)


def mixer(h, conv_hist, h0, attend, lw, lam_init):
    B, L = h.shape[0], h.shape[1]
    proj = h @ lw['w_in']
    cuts = [D_RNN, 2 * D_RNN, 2 * D_RNN + D_ATT, 2 * D_RNN + 2 * D_ATT,
            2 * D_RNN + 2 * D_ATT + D_ATT_V, 2 * D_RNN + 2 * D_ATT + D_ATT_V + D_MODEL]
    xr, yr, q, k, v, gr, ga = jnp.split(proj, cuts, axis=-1)
    conv_in = jnp.concatenate([conv_hist.astype(xr.dtype), xr], axis=1)
    xc = causal_conv(conv_in, lw['conv_w'], lw['conv_b'])
    hr, h_last = rg_lru(xc, h0, lw['w_a'], lw['b_a'], lw['w_x'], lw['b_x'], lw['lru_lambda'])
    rnn_out = hr * jax.nn.gelu(yr, approximate=True)
    q = q.reshape(B, L, N_DH, 2, HEAD_DIM)
    k = k.reshape(B, L, N_DH, 2, HEAD_DIM)
    v = v.reshape(B, L, N_DH, V_DIM)
    lam = (jnp.exp(jnp.sum(lw['lq1'] * lw['lk1'])) - jnp.exp(jnp.sum(lw['lq2'] * lw['lk2']))
           + lam_init).astype(jnp.float32)
    o = attend(q, k, v, lam)
    o = rms_norm(o, lw['subln_g']) * (1.0 - lam_init)
    merged = (jax.nn.sigmoid(gr) * (rnn_out @ lw['w_pr'])
              + jax.nn.sigmoid(ga) * (o.reshape(B, L, D_ATT_V) @ lw['w_pa']))
    out = merged @ lw['w_out']
    return out, k, v, h_last, conv_in[:, -(CONV_WIDTH - 1):]


def trunk(x, c, conv_hists, h0s, attend_for_layer, P):
    ks, vs, hs, cs = [], [], [], []
    for l in range(DEPTH):
        lw = {name: arr[l] for name, arr in P.items() if name != 'final_g'}
        lam_init = 0.8 - 0.6 * math.exp(-0.3 * l)
        sh1, sc1, g1, sh2, sc2, g2 = ada_mod(c, lw['w_ada'], lw['b_ada'])
        hn = rms_norm(x, lw['norm1_g']) * (1.0 + sc1) + sh1
        out, k, v, hl, cv = mixer(hn, conv_hists(l), h0s(l), attend_for_layer(l), lw, lam_init)
        x = x + g1 * out
        hn = rms_norm(x, lw['norm2_g']) * (1.0 + sc2) + sh2
        x = x + g2 * (jnp.square(jax.nn.relu(hn @ lw['w_up'])) @ lw['w_down'])
        ks.append(k); vs.append(v); hs.append(hl); cs.append(cv)
    y = rms_norm(x, P['final_g'])
    return y, jnp.stack(ks), jnp.stack(vs), jnp.stack(hs), jnp.stack(cs)


def setup_inputs(seed: int = 0) -> dict:
    key = jax.random.key(seed)
    ks = iter(jax.random.split(key, 48))
    n_pages = PAST_LEN // PAGE_SIZE
    n_pool = (DEC_BATCH * n_pages * 5) // 4

    def nrm(shape, scale=1.0):
        return jax.random.normal(next(ks), shape, jnp.float32) * scale

    d_in = 2 * D_RNN + 2 * D_ATT + D_ATT_V + 2 * D_MODEL
    a8 = jax.random.uniform(next(ks), (DEPTH, D_RNN), jnp.float32, 0.9, 0.999)
    s = a8 ** (1.0 / LRU_C)
    lru_lambda = jnp.log(s) - jnp.log1p(-s)
    page_table = jax.random.permutation(next(ks), n_pool)[:DEC_BATCH * n_pages]
    page_table = page_table.reshape(DEC_BATCH, n_pages).astype(jnp.int32)
    return {
        'x_prompt': nrm((BATCH, SEQ, D_MODEL)),
        'x_sample': nrm((DEC_BATCH, DEC_SEQ, D_MODEL)),
        'c_prompt': nrm((BATCH, D_MODEL)),
        'c_sample': nrm((DEC_BATCH, D_MODEL)),
        'cache_k': nrm((DEPTH, n_pool, PAGE_SIZE, N_DH, 2, HEAD_DIM)),
        'cache_v': nrm((DEPTH, n_pool, PAGE_SIZE, N_DH, V_DIM)),
        'state_rnn': nrm((DEPTH, DEC_BATCH, D_RNN), 0.5),
        'state_conv': nrm((DEPTH, DEC_BATCH, CONV_WIDTH - 1, D_RNN)),
        'page_table': page_table,
        'w_ada': nrm((DEPTH, D_MODEL, 6 * D_MODEL), 0.5 * D_MODEL ** -0.5),
        'b_ada': nrm((DEPTH, 6 * D_MODEL), 0.02),
        'norm1_g': 1.0 + nrm((DEPTH, D_MODEL), 0.02),
        'norm2_g': 1.0 + nrm((DEPTH, D_MODEL), 0.02),
        'w_in': nrm((DEPTH, D_MODEL, d_in), D_MODEL ** -0.5),
        'conv_w': nrm((DEPTH, CONV_WIDTH, D_RNN), CONV_WIDTH ** -0.5),
        'conv_b': nrm((DEPTH, D_RNN), 0.02),
        'w_a': nrm((DEPTH, N_RNN_BLOCKS, RNN_BLOCK, RNN_BLOCK), RNN_BLOCK ** -0.5),
        'b_a': nrm((DEPTH, D_RNN), 0.02),
        'w_x': nrm((DEPTH, N_RNN_BLOCKS, RNN_BLOCK, RNN_BLOCK), RNN_BLOCK ** -0.5),
        'b_x': nrm((DEPTH, D_RNN), 0.02),
        'lru_lambda': lru_lambda,
        'lq1': nrm((DEPTH, HEAD_DIM), 0.1),
        'lk1': nrm((DEPTH, HEAD_DIM), 0.1),
        'lq2': nrm((DEPTH, HEAD_DIM), 0.1),
        'lk2': nrm((DEPTH, HEAD_DIM), 0.1),
        'subln_g': 1.0 + nrm((DEPTH, V_DIM), 0.02),
        'w_pr': nrm((DEPTH, D_RNN, D_MODEL), D_RNN ** -0.5),
        'w_pa': nrm((DEPTH, D_ATT_V, D_MODEL), D_ATT_V ** -0.5),
        'w_out': nrm((DEPTH, D_MODEL, D_MODEL), D_MODEL ** -0.5),
        'w_up': nrm((DEPTH, D_MODEL, D_FF), D_MODEL ** -0.5),
        'w_down': nrm((DEPTH, D_FF, D_MODEL), D_FF ** -0.5),
        'final_g': 1.0 + nrm((D_MODEL,), 0.02),
    }


def reference(x_prompt, x_sample, c_prompt, c_sample, cache_k, cache_v, state_rnn, state_conv,
              page_table, w_ada, b_ada, norm1_g, norm2_g, w_in, conv_w, conv_b, w_a, b_a, w_x, b_x,
              lru_lambda, lq1, lk1, lq2, lk2, subln_g, w_pr, w_pa, w_out, w_up, w_down, final_g):
    P = {'w_ada': w_ada, 'b_ada': b_ada, 'norm1_g': norm1_g, 'norm2_g': norm2_g, 'w_in': w_in,
         'conv_w': conv_w, 'conv_b': conv_b, 'w_a': w_a, 'b_a': b_a, 'w_x': w_x, 'b_x': b_x,
         'lru_lambda': lru_lambda, 'lq1': lq1, 'lk1': lk1, 'lq2': lq2, 'lk2': lk2,
         'subln_g': subln_g, 'w_pr': w_pr, 'w_pa': w_pa, 'w_out': w_out, 'w_up': w_up,
         'w_down': w_down, 'final_g': final_g}
    n_past = page_table.shape[1] * PAGE_SIZE

    zero_conv = jnp.zeros((x_prompt.shape[0], CONV_WIDTH - 1, D_RNN), x_prompt.dtype)
    zero_h = jnp.zeros((x_prompt.shape[0], D_RNN), jnp.float32)
    y_prompt, k_p, v_p, h_p, cv_p = trunk(
        x_prompt, c_prompt, lambda l: zero_conv, lambda l: zero_h,
        lambda l: diff_attn_prompt, P)

    def sample_attend(l):
        def attend(q, k, v, lam):
            kp = cache_k[l, page_table].reshape(page_table.shape[0], n_past, N_DH, 2, HEAD_DIM)
            vp = cache_v[l, page_table].reshape(page_table.shape[0], n_past, N_DH, V_DIM)
            return diff_attn_sample(q, k, v, kp, vp, lam)
        return attend

    y_sample, k_s, v_s, h_s, cv_s = trunk(
        x_sample, c_sample, lambda l: state_conv[l], lambda l: state_rnn[l], sample_attend, P)

    return (y_prompt, y_sample, k_p, v_p, h_p, cv_p, k_s, v_s, h_s, cv_s)
```

```python
import functools
import math

import jax
import jax.numpy as jnp
from jax import lax
from jax.experimental import pallas as pl
from jax.experimental.pallas import tpu as pltpu

F32 = jnp.float32
BF16 = jnp.bfloat16

EPS = 1e-6
LRU_C = 8.0
N_RNN_BLOCKS = 8
CONV_WIDTH = 4
N_DH = 4
HEAD_DIM = 64
V_DIM = 2 * HEAD_DIM
N_MAPS = 2 * N_DH
PAGE_SIZE = 128
MASK_VALUE = -1e30

V7X_VMEM_BYTES = 64 * 1024 * 1024
VMEM_LIMIT_BYTES = V7X_VMEM_BYTES - 8 * 1024 * 1024
SUBLANES = 8
LANES = 128

ROW_TILE = 512
RNN_CHUNK = 256
ATTN_TILE = 256
DECODE_PAGES = 8
ADA_TILE = 1536
FF_CHUNK = 1024


def _params(semantics):
    return pltpu.CompilerParams(dimension_semantics=semantics, vmem_limit_bytes=VMEM_LIMIT_BYTES)


def _rms_norm(x, g):
    return x * lax.rsqrt(jnp.mean(x * x, axis=-1, keepdims=True) + EPS) * g


def _lam_value(lq1_ref, lk1_ref, lq2_ref, lk2_ref, lam_init):
    s1 = jnp.sum(lq1_ref[...] * lk1_ref[...], axis=1, keepdims=True)
    s2 = jnp.sum(lq2_ref[...] * lk2_ref[...], axis=1, keepdims=True)
    return jnp.exp(s1) - jnp.exp(s2) + lam_init


def _ada_kernel(c_ref, w_ref, b_ref, o_ref):
    c = c_ref[...]
    a = (c * jax.nn.sigmoid(c)).astype(BF16)
    o_ref[...] = jnp.dot(a, w_ref[...].astype(BF16), preferred_element_type=F32) + b_ref[...]


def _ada_mod(c_all, w_ada, b_ada):
    depth, d, n = w_ada.shape
    rows = c_all.shape[0]
    return pl.pallas_call(
        _ada_kernel,
        grid=(depth, n // ADA_TILE),
        in_specs=[
            pl.BlockSpec((rows, d), lambda l, j: (0, 0)),
            pl.BlockSpec((None, d, ADA_TILE), lambda l, j: (l, 0, j)),
            pl.BlockSpec((None, 1, ADA_TILE), lambda l, j: (l, 0, j)),
        ],
        out_specs=pl.BlockSpec((None, rows, ADA_TILE), lambda l, j: (l, 0, j)),
        out_shape=jax.ShapeDtypeStruct((depth, rows, n), F32),
        compiler_params=_params(("arbitrary", "arbitrary")),
    )(c_all, w_ada, b_ada.reshape(depth, 1, n))


def _mod_spec(layer, which, rows, d):
    return pl.BlockSpec((None, None, rows, d), lambda g, i: (layer, g, 0, which))


def _stream_columns(d):
    widths = [d, d, N_DH * 2 * HEAD_DIM, N_DH * 2 * HEAD_DIM, N_DH * V_DIM, d, d]
    los = [sum(widths[:i]) for i in range(len(widths))]
    return [(lo, lo + w) for lo, w in zip(los, widths)]


def _in_proj_kernel(x_ref, sh_ref, sc_ref, g_ref, w_ref, *refs, prompt):
    hn = _rms_norm(x_ref[...], g_ref[...]) * (1.0 + sc_ref[...]) + sh_ref[...]
    hb = hn.astype(BF16)
    cols = _stream_columns(x_ref.shape[1])
    stream = lambda idx: jnp.dot(hb, w_ref[:, cols[idx][0]:cols[idx][1]], preferred_element_type=F32)
    q_scale = HEAD_DIM ** -0.5
    if prompt:
        wkt_ref, xr_ref, yr_ref, q_ref, kt_ref, v_ref, gr_ref, ga_ref, ktb_ref, vb_ref = refs
        kt = lax.dot_general(wkt_ref[...], hb, (((1,), (1,)), ((), ())), preferred_element_type=F32)
        kt_ref[...] = kt
        ktb_ref[...] = kt.astype(BF16)
        v = stream(4)
        v_ref[...] = v
        vb_ref[...] = v.astype(BF16)
    else:
        xr_ref, yr_ref, q_ref, k_ref, v_ref, gr_ref, ga_ref = refs
        k_ref[...] = stream(3)
        v_ref[...] = stream(4)
    xr_ref[...] = stream(0)
    yr_ref[...] = stream(1).astype(yr_ref.dtype)
    q_ref[...] = (stream(2) * q_scale).astype(q_ref.dtype)
    gr_ref[...] = stream(5).astype(gr_ref.dtype)
    ga_ref[...] = stream(6).astype(ga_ref.dtype)


def _in_proj(x, mod, layer, norm_g, w_in, w_kt, *, tm, prompt):
    groups, rows, d = x.shape
    mod_rows = mod.shape[2]
    widths = [hi - lo for lo, hi in _stream_columns(d)]
    inter = BF16 if prompt else F32
    dtypes = [F32, inter, inter, F32, F32, inter, inter]
    tile = lambda w: pl.BlockSpec((None, tm, w), lambda g, i: (g, i, 0))
    tile_t = pl.BlockSpec((None, widths[3], tm), lambda g, i: (g, 0, i))
    out_shapes = [jax.ShapeDtypeStruct((groups, rows, w), t) for w, t in zip(widths, dtypes)]
    out_specs = [tile(w) for w in widths]
    in_specs = [
        tile(d),
        _mod_spec(layer, 0, mod_rows, d),
        _mod_spec(layer, 1, mod_rows, d),
        pl.BlockSpec((None, 1, d), lambda g, i: (layer, 0, 0)),
        pl.BlockSpec((None, d, w_in.shape[2]), lambda g, i: (layer, 0, 0)),
    ]
    args = [x, mod, mod, norm_g, w_in]
    if prompt:
        in_specs.append(pl.BlockSpec((None,) + w_kt.shape[1:], lambda g, i: (layer, 0, 0)))
        args.append(w_kt)
        out_shapes[3] = jax.ShapeDtypeStruct((groups, widths[3], rows), F32)
        out_specs[3] = tile_t
        out_shapes += [jax.ShapeDtypeStruct((groups, widths[3], rows), BF16),
                       jax.ShapeDtypeStruct((groups, rows, widths[4]), BF16)]
        out_specs += [tile_t, tile(widths[4])]
    return pl.pallas_call(
        functools.partial(_in_proj_kernel, prompt=prompt),
        grid=(groups, rows // tm),
        in_specs=in_specs,
        out_specs=out_specs,
        out_shape=out_shapes,
        compiler_params=_params(("parallel", "parallel")),
    )(*args)


def _softplus(z):
    return jnp.maximum(z, 0.0) + jnp.log1p(jnp.exp(-jnp.abs(z)))


def _block_diag_dot(xb, w_ref):
    blk = xb.shape[1] // N_RNN_BLOCKS
    return jnp.concatenate(
        [jnp.dot(xb[:, n * blk:(n + 1) * blk], w_ref[n], preferred_element_type=F32)
         for n in range(N_RNN_BLOCKS)], axis=1)


def _lru_terms(xc, wa_ref, ba_ref, wx_ref, bx_ref, lam_ref):
    xb = xc.astype(BF16)
    r = jax.nn.sigmoid(_block_diag_dot(xb, wa_ref) + ba_ref[...])
    i = jax.nn.sigmoid(_block_diag_dot(xb, wx_ref) + bx_ref[...])
    log_a = -LRU_C * r * _softplus(-lam_ref[...])
    a = jnp.exp(log_a)
    b = jnp.sqrt(-jnp.tanh(log_a) * (a * a + 1.0)) * (i * xc)
    return a, b


def _gelu_tanh(y):
    return 0.5 * y * (1.0 + jnp.tanh(math.sqrt(2.0 / math.pi) * (y + 0.044715 * (y * y * y))))


def _rnn_prompt_kernel(xr_ref, yr_ref, cw_ref, cb_ref, wa_ref, ba_ref, wx_ref, bx_ref, lam_ref,
                       out_ref, hlast_ref, conv_ref, xbuf, h_scr):
    t = pl.program_id(1)
    steps = xr_ref.shape[0]
    hist = SUBLANES

    @pl.when(t == 0)
    def _():
        xbuf[0:hist, :] = jnp.zeros((hist, xbuf.shape[1]), F32)
        h_scr[...] = jnp.zeros_like(h_scr)

    x = xr_ref[...]
    xbuf[hist:hist + steps, :] = x
    xc = cb_ref[...] + cw_ref[CONV_WIDTH - 1:CONV_WIDTH, :] * x
    for j in range(CONV_WIDTH - 1):
        back = CONV_WIDTH - 1 - j
        xc = xc + cw_ref[j:j + 1, :] * xbuf[hist - back:hist - back + steps, :]
    xbuf[0:hist, :] = x[steps - hist:steps, :]

    a, b = _lru_terms(xc, wa_ref, ba_ref, wx_ref, bx_ref, lam_ref)
    row = lax.broadcasted_iota(jnp.int32, (steps, 1), 0)
    shift = 1
    while shift < steps:
        a_prev = pltpu.roll(a, shift, axis=0)
        b_prev = pltpu.roll(b, shift, axis=0)
        valid = row >= shift
        b = jnp.where(valid, a * b_prev + b, b)
        a = jnp.where(valid, a * a_prev, a)
        shift *= 2
    h = a * h_scr[...] + b
    h_scr[...] = h[steps - 1:steps, :]
    out_ref[...] = (h * _gelu_tanh(yr_ref[...].astype(F32))).astype(out_ref.dtype)

    @pl.when(t == pl.num_programs(1) - 1)
    def _():
        hlast_ref[...] = h[steps - 1:steps, :]
        conv_ref[...] = x[steps - (CONV_WIDTH - 1):steps, :]


def _rnn_weight_specs(layer, c, grid_rank):
    z = (0,) * 0
    idx2 = lambda *_: (layer, 0, 0)
    idx3 = lambda *_: (layer, 0, 0, 0)
    blk = c // N_RNN_BLOCKS
    return [
        pl.BlockSpec((None, CONV_WIDTH, c), idx2),
        pl.BlockSpec((None, 1, c), idx2),
        pl.BlockSpec((None, N_RNN_BLOCKS, blk, blk), idx3),
        pl.BlockSpec((None, 1, c), idx2),
        pl.BlockSpec((None, N_RNN_BLOCKS, blk, blk), idx3),
        pl.BlockSpec((None, 1, c), idx2),
        pl.BlockSpec((None, 1, c), idx2),
    ]


def _rnn_prompt(xr, yr, layer, rnn_w):
    batch, seq, c = xr.shape
    tile = pl.BlockSpec((None, RNN_CHUNK, c), lambda b, t: (b, t, 0))
    return pl.pallas_call(
        _rnn_prompt_kernel,
        grid=(batch, seq // RNN_CHUNK),
        in_specs=[tile, tile] + _rnn_weight_specs(layer, c, 2),
        out_specs=[
            tile,
            pl.BlockSpec((None, 1, c), lambda b, t: (b, 0, 0)),
            pl.BlockSpec((None, CONV_WIDTH - 1, c), lambda b, t: (b, 0, 0)),
        ],
        out_shape=[
            jax.ShapeDtypeStruct((batch, seq, c), BF16),
            jax.ShapeDtypeStruct((batch, 1, c), F32),
            jax.ShapeDtypeStruct((batch, CONV_WIDTH - 1, c), F32),
        ],
        scratch_shapes=[pltpu.VMEM((SUBLANES + RNN_CHUNK, c), F32), pltpu.VMEM((1, c), F32)],
        compiler_params=_params(("parallel", "arbitrary")),
    )(xr, yr, *rnn_w)


def _rnn_sample_kernel(xr_ref, yr_ref, hist_ref, h0_ref, cw_ref, cb_ref, wa_ref, ba_ref, wx_ref,
                       bx_ref, lam_ref, out_ref, hnew_ref, conv_ref):
    x = xr_ref[...]
    xc = cb_ref[...] + cw_ref[CONV_WIDTH - 1:CONV_WIDTH, :] * x
    for j in range(CONV_WIDTH - 1):
        xc = xc + cw_ref[j:j + 1, :] * hist_ref[j]
    a, b = _lru_terms(xc, wa_ref, ba_ref, wx_ref, bx_ref, lam_ref)
    h = a * h0_ref[...] + b
    hnew_ref[...] = h
    out_ref[...] = h * _gelu_tanh(yr_ref[...])
    for j in range(CONV_WIDTH - 2):
        conv_ref[j] = hist_ref[j + 1]
    conv_ref[CONV_WIDTH - 2] = x


def _rnn_sample(xr, yr, hist_t, h0, layer, rnn_w):
    n, c = xr.shape
    full = pl.BlockSpec((n, c), lambda i: (0, 0))
    hist_spec = pl.BlockSpec((None, CONV_WIDTH - 1, n, c), lambda i: (layer, 0, 0, 0))
    return pl.pallas_call(
        _rnn_sample_kernel,
        grid=(1,),
        in_specs=[full, full, hist_spec, pl.BlockSpec((None, n, c), lambda i: (layer, 0, 0))]
        + _rnn_weight_specs(layer, c, 1),
        out_specs=[full, full, pl.BlockSpec((CONV_WIDTH - 1, n, c), lambda i: (0, 0, 0))],
        out_shape=[
            jax.ShapeDtypeStruct((n, c), F32),
            jax.ShapeDtypeStruct((n, c), F32),
            jax.ShapeDtypeStruct((CONV_WIDTH - 1, n, c), F32),
        ],
        compiler_params=_params(("arbitrary",)),
    )(xr, yr, hist_t, h0, *rnn_w)


def _attn_prompt_kernel(q_ref, k_ref, v_ref, lq1_ref, lk1_ref, lq2_ref, lk2_ref, g_ref, o_ref,
                        m_scr, l_scr, acc_scr, *, lam_init):
    i = pl.program_id(2)
    tq = q_ref.shape[0]
    tk = tq
    q = q_ref[...]
    lane = lax.broadcasted_iota(jnp.int32, q.shape, 1)
    zero = jnp.zeros_like(q)
    qq = jnp.concatenate([jnp.where(lane < HEAD_DIM, q, zero), jnp.where(lane >= HEAD_DIM, q, zero)],
                         axis=0)
    m_scr[...] = jnp.full(m_scr.shape, MASK_VALUE, F32)
    l_scr[...] = jnp.zeros_like(l_scr)
    acc_scr[...] = jnp.zeros_like(acc_scr)

    def step(j, masked):
        start = pl.multiple_of(j * tk, tk)
        kj = k_ref[:, pl.ds(start, tk)]
        vj = v_ref[pl.ds(start, tk), :]
        s = jnp.dot(qq, kj, preferred_element_type=F32)
        if masked:
            r = lax.broadcasted_iota(jnp.int32, s.shape, 0)
            c = lax.broadcasted_iota(jnp.int32, s.shape, 1)
            qpos = jnp.where(r >= tq, r - tq, r)
            s = jnp.where(c <= qpos, s, MASK_VALUE)
        m_prev = m_scr[...]
        m_next = jnp.maximum(m_prev, jnp.max(s, axis=1, keepdims=True))
        alpha = jnp.exp(m_prev - m_next)
        p = jnp.exp(s - pltpu.repeat(m_next, tk // LANES, axis=1))
        l_scr[...] = alpha * l_scr[...] + jnp.sum(p, axis=1, keepdims=True)
        acc_scr[...] = alpha * acc_scr[...] + jnp.dot(p.astype(BF16), vj, preferred_element_type=F32)
        m_scr[...] = m_next

    def body(j, carry):
        step(j, False)
        return carry

    lax.fori_loop(0, i, body, 0)
    step(i, True)

    o = acc_scr[...] / l_scr[...]
    lam = _lam_value(lq1_ref, lk1_ref, lq2_ref, lk2_ref, lam_init)
    o = o[:tq, :] - lam * o[tq:, :]
    o_ref[...] = (_rms_norm(o, g_ref[...]) * (1.0 - lam_init)).astype(o_ref.dtype)


def _lam_specs(layer, index_rank):
    idx = lambda *_: (layer, 0, 0)
    return [pl.BlockSpec((None, 1, HEAD_DIM), idx)] * 4 + [pl.BlockSpec((None, 1, V_DIM), idx)]


def _attn_prompt(q, k, v, layer, lam_w, lam_init):
    batch, seq, _ = q.shape
    tq = ATTN_TILE
    q_spec = pl.BlockSpec((None, tq, V_DIM), lambda b, h, i: (b, i, h))
    k_spec = pl.BlockSpec((None, 2 * HEAD_DIM, seq), lambda b, h, i: (b, h, 0))
    v_spec = pl.BlockSpec((None, seq, V_DIM), lambda b, h, i: (b, 0, h))
    return pl.pallas_call(
        functools.partial(_attn_prompt_kernel, lam_init=lam_init),
        grid=(batch, N_DH, seq // tq),
        in_specs=[q_spec, k_spec, v_spec] + _lam_specs(layer, 3),
        out_specs=q_spec,
        out_shape=jax.ShapeDtypeStruct((batch, seq, N_DH * V_DIM), BF16),
        scratch_shapes=[pltpu.VMEM((2 * tq, LANES), F32), pltpu.VMEM((2 * tq, LANES), F32),
                        pltpu.VMEM((2 * tq, V_DIM), F32)],
        compiler_params=_params(("parallel", "parallel", "arbitrary")),
    )(q, k, v, *lam_w)


def _attn_decode_kernel(pt_ref, q_ref, kn_ref, vn_ref, lq1_ref, lk1_ref, lq2_ref, lk2_ref, g_ref,
                        *refs, lam_init, pages):
    k_refs = refs[:pages]
    v_refs = refs[pages:2 * pages]
    o_ref = refs[2 * pages]
    m_scr, l_scr, acc_scr = refs[2 * pages + 1:]
    c = pl.program_id(1)
    width = q_ref.shape[1]

    @pl.when(c == 0)
    def _():
        m_scr[...] = jnp.full(m_scr.shape, MASK_VALUE, F32)
        l_scr[...] = jnp.zeros_like(l_scr)
        acc_scr[...] = jnp.zeros_like(acc_scr)

    map_row = lax.broadcasted_iota(jnp.int32, (N_MAPS, width), 0)
    lane = lax.broadcasted_iota(jnp.int32, (N_MAPS, width), 1)
    qbd = jnp.where(lane // HEAD_DIM == map_row, jnp.broadcast_to(q_ref[...], (N_MAPS, width)), 0.0)
    qb = qbd.astype(BF16)

    s = jnp.concatenate(
        [jnp.dot(qb, k_refs[i][...].astype(BF16), preferred_element_type=F32) for i in range(pages)],
        axis=1)
    m_prev = m_scr[...]
    m_next = jnp.maximum(m_prev, jnp.max(s, axis=1, keepdims=True))
    alpha = jnp.exp(m_prev - m_next)
    p = jnp.exp(s - pltpu.repeat(m_next, pages, axis=1))
    l_scr[...] = alpha * l_scr[...] + jnp.sum(p, axis=1, keepdims=True)
    pb = p.astype(BF16)
    heads = []
    for h in range(N_DH):
        pv = None
        for i in range(pages):
            v_h = v_refs[i][pl.ds(h, PAGE_SIZE, stride=N_DH), :].astype(BF16)
            part = jnp.dot(pb[:, i * PAGE_SIZE:(i + 1) * PAGE_SIZE], v_h, preferred_element_type=F32)
            pv = part if pv is None else pv + part
        heads.append(pv)
    acc_scr[...] = (pltpu.repeat(alpha, width // LANES, axis=1) * acc_scr[...]
                    + jnp.concatenate(heads, axis=1))
    m_scr[...] = m_next

    @pl.when(c == pl.num_programs(1) - 1)
    def _():
        s_new = jnp.sum(qbd * kn_ref[...], axis=1, keepdims=True)
        m_prev = m_scr[...]
        m_fin = jnp.maximum(m_prev, s_new)
        alpha = jnp.exp(m_prev - m_fin)
        p_new = jnp.exp(s_new - m_fin)
        l_fin = alpha * l_scr[...] + p_new
        acc = (pltpu.repeat(alpha, width // LANES, axis=1) * acc_scr[...]
               + pltpu.repeat(p_new, width // LANES, axis=1) * vn_ref[...])
        res = acc / pltpu.repeat(l_fin, width // LANES, axis=1)
        head = lane // V_DIM
        first = jnp.sum(jnp.where(map_row == 2 * head, res, 0.0), axis=0, keepdims=True)
        second = jnp.sum(jnp.where(map_row == 2 * head + 1, res, 0.0), axis=0, keepdims=True)
        lam = _lam_value(lq1_ref, lk1_ref, lq2_ref, lk2_ref, lam_init)
        o = first - lam * second
        g = g_ref[...]
        o_ref[...] = jnp.concatenate(
            [_rms_norm(o[:, h * V_DIM:(h + 1) * V_DIM], g) for h in range(N_DH)],
            axis=1) * (1.0 - lam_init)


def _attn_decode(q, k_new, v_new, cache_k, cache_v, page_table, layer, lam_w, lam_init):
    n, _, width = q.shape
    n_pages = page_table.shape[1]
    pages = DECODE_PAGES
    row = pl.BlockSpec((None, 1, width), lambda b, c, pt: (b, 0, 0))
    lam_specs = [pl.BlockSpec((None, 1, HEAD_DIM), lambda b, c, pt: (layer, 0, 0))] * 4
    lam_specs.append(pl.BlockSpec((None, 1, V_DIM), lambda b, c, pt: (layer, 0, 0)))

    def page_spec(i, shape):
        return pl.BlockSpec((None, None) + shape,
                            lambda b, c, pt: (layer, pt[b * n_pages + c * pages + i], 0, 0))

    k_pages = [page_spec(i, (width, PAGE_SIZE)) for i in range(pages)]
    v_pages = [page_spec(i, (PAGE_SIZE * N_DH, V_DIM)) for i in range(pages)]

    grid_spec = pltpu.PrefetchScalarGridSpec(
        num_scalar_prefetch=1,
        grid=(n, n_pages // pages),
        in_specs=[row, row, row] + lam_specs + k_pages + v_pages,
        out_specs=row,
        scratch_shapes=[pltpu.VMEM((N_MAPS, LANES), F32), pltpu.VMEM((N_MAPS, LANES), F32),
                        pltpu.VMEM((N_MAPS, width), F32)],
    )
    return pl.pallas_call(
        functools.partial(_attn_decode_kernel, lam_init=lam_init, pages=pages),
        grid_spec=grid_spec,
        out_shape=jax.ShapeDtypeStruct((n, 1, width), F32),
        compiler_params=_params(("parallel", "arbitrary")),
    )(page_table.reshape(-1), q, k_new, v_new, *lam_w, *([cache_k] * pages), *([cache_v] * pages))


def _merge_kernel(x_ref, rnn_ref, att_ref, gr_ref, ga_ref, g1_ref, wpr_ref, wpa_ref, wout_ref, o_ref):
    pr = jnp.dot(rnn_ref[...].astype(BF16), wpr_ref[...], preferred_element_type=F32)
    pa = jnp.dot(att_ref[...].astype(BF16), wpa_ref[...], preferred_element_type=F32)
    merged = (jax.nn.sigmoid(gr_ref[...].astype(F32)) * pr
              + jax.nn.sigmoid(ga_ref[...].astype(F32)) * pa)
    out = jnp.dot(merged.astype(BF16), wout_ref[...], preferred_element_type=F32)
    o_ref[...] = x_ref[...] + g1_ref[...] * out


def _merge(x, rnn_out, att, gr, ga, mod, layer, w_pr, w_pa, w_out, *, tm):
    groups, rows, d = x.shape
    tile = lambda a: pl.BlockSpec((None, tm, a.shape[2]), lambda g, i: (g, i, 0))
    weight = lambda w: pl.BlockSpec((None,) + w.shape[1:], lambda g, i: (layer, 0, 0))
    return pl.pallas_call(
        _merge_kernel,
        grid=(groups, rows // tm),
        in_specs=[tile(x), tile(rnn_out), tile(att), tile(gr), tile(ga),
                  _mod_spec(layer, 2, mod.shape[2], d), weight(w_pr), weight(w_pa), weight(w_out)],
        out_specs=tile(x),
        out_shape=jax.ShapeDtypeStruct(x.shape, F32),
        compiler_params=_params(("parallel", "parallel")),
    )(x, rnn_out, att, gr, ga, mod, w_pr, w_pa, w_out)


def _mlp_kernel(x_ref, sh_ref, sc_ref, g2_ref, ng_ref, wup_ref, wdn_ref, *rest, final):
    x = x_ref[...]
    hb = (_rms_norm(x, ng_ref[...]) * (1.0 + sc_ref[...]) + sh_ref[...]).astype(BF16)
    d_ff = wup_ref.shape[1]
    acc = jnp.zeros(x.shape, F32)
    for lo in range(0, d_ff, FF_CHUNK):
        u = jnp.maximum(jnp.dot(hb, wup_ref[:, lo:lo + FF_CHUNK], preferred_element_type=F32), 0.0)
        acc = acc + jnp.dot((u * u).astype(BF16), wdn_ref[lo:lo + FF_CHUNK, :],
                            preferred_element_type=F32)
    y = x + g2_ref[...] * acc
    if final:
        fg_ref, o_ref = rest
        o_ref[...] = _rms_norm(y, fg_ref[...])
    else:
        (o_ref,) = rest
        o_ref[...] = y


def _mlp(x, mod, layer, norm_g, w_up, w_down, final_g, *, tm, final):
    groups, rows, d = x.shape
    tile = pl.BlockSpec((None, tm, d), lambda g, i: (g, i, 0))
    weight = lambda w: pl.BlockSpec((None,) + w.shape[1:], lambda g, i: (layer, 0, 0))
    mod_rows = mod.shape[2]
    in_specs = [tile, _mod_spec(layer, 3, mod_rows, d), _mod_spec(layer, 4, mod_rows, d),
                _mod_spec(layer, 5, mod_rows, d), pl.BlockSpec((None, 1, d), lambda g, i: (layer, 0, 0)),
                weight(w_up), weight(w_down)]
    args = [x, mod, mod, mod, norm_g, w_up, w_down]
    if final:
        in_specs.append(pl.BlockSpec((1, d), lambda g, i: (0, 0)))
        args.append(final_g)
    return pl.pallas_call(
        functools.partial(_mlp_kernel, final=final),
        grid=(groups, rows // tm),
        in_specs=in_specs,
        out_specs=tile,
        out_shape=jax.ShapeDtypeStruct(x.shape, F32),
        compiler_params=_params(("parallel", "parallel")),
    )(*args)


def kernel(x_prompt, x_sample, c_prompt, c_sample, cache_k, cache_v, state_rnn, state_conv, page_table, w_ada, b_ada, norm1_g, norm2_g, w_in, conv_w, conv_b, w_a, b_a, w_x, b_x, lru_lambda, lq1, lk1, lq2, lk2, subln_g, w_pr, w_pa, w_out, w_up, w_down, final_g):
    depth, d = norm1_g.shape
    batch, seq, _ = x_prompt.shape
    n_dec = x_sample.shape[0]
    d_att = N_DH * 2 * HEAD_DIM
    assert x_sample.shape[1] == 1, "one new token per running sequence"
    assert seq % ROW_TILE == 0 and seq % RNN_CHUNK == 0 and seq % ATTN_TILE == 0
    assert page_table.shape[1] % DECODE_PAGES == 0 and cache_k.shape[2] == PAGE_SIZE

    w_in_b, w_pr_b, w_pa_b, w_out_b, w_up_b, w_down_b, w_a_b, w_x_b = (
        w.astype(BF16) for w in (w_in, w_pr, w_pa, w_out, w_up, w_down, w_a, w_x))
    row3 = lambda a: a.reshape(depth, 1, a.shape[-1])
    rnn_w = (conv_w, row3(conv_b), w_a_b, row3(b_a), w_x_b, row3(b_x), row3(lru_lambda))
    lam_w = (row3(lq1), row3(lk1), row3(lq2), row3(lk2), row3(subln_g))
    norm1, norm2 = row3(norm1_g), row3(norm2_g)
    final_g2 = final_g.reshape(1, d)

    n_cond = n_dec + batch
    pad = -n_cond % (2 * SUBLANES)
    c_all = jnp.concatenate([c_sample, c_prompt, jnp.zeros((pad, d), F32)], axis=0)
    mods = _ada_mod(c_all, w_ada, b_ada)
    mod_s = mods[:, :n_dec].reshape(depth, 1, n_dec, 6 * d)
    mod_p = mods[:, n_dec:n_cond].reshape(depth, batch, 1, 6 * d)

    ck = jnp.transpose(cache_k, (0, 1, 3, 4, 5, 2)).reshape(depth, cache_k.shape[1], d_att, PAGE_SIZE)
    cv = cache_v.reshape(depth, cache_v.shape[1], PAGE_SIZE * N_DH, V_DIM)
    hist_t = jnp.swapaxes(state_conv, 1, 2)
    cols = _stream_columns(d)
    w_kt = jnp.swapaxes(w_in[:, :, cols[3][0]:cols[3][1]], 1, 2).astype(BF16)

    xp = x_prompt
    xs = x_sample.reshape(1, n_dec, d)
    k_p, v_p, h_p, cv_p, k_s, v_s, h_s, cv_s = ([] for _ in range(8))
    for l in range(depth):
        lam_init = 0.8 - 0.6 * math.exp(-0.3 * l)
        last = l == depth - 1

        xr, yr, q, kt, v, gr, ga, ktb, vb = _in_proj(xp, mod_p, l, norm1, w_in_b, w_kt, tm=ROW_TILE,
                                                     prompt=True)
        rnn_out, h_last, conv_last = _rnn_prompt(xr, yr, l, rnn_w)
        att = _attn_prompt(q, ktb, vb, l, lam_w, lam_init)
        xp = _merge(xp, rnn_out, att, gr, ga, mod_p, l, w_pr_b, w_pa_b, w_out_b, tm=ROW_TILE)
        xp = _mlp(xp, mod_p, l, norm2, w_up_b, w_down_b, final_g2, tm=ROW_TILE, final=last)
        k_p.append(kt); v_p.append(v); h_p.append(h_last[:, 0]); cv_p.append(conv_last)

        xr, yr, q, k, v, gr, ga = _in_proj(xs, mod_s, l, norm1, w_in_b, None, tm=n_dec, prompt=False)
        rnn_out, h_new, conv_new = _rnn_sample(xr[0], yr[0], hist_t, state_rnn, l, rnn_w)
        as_rows = lambda a: a.reshape(n_dec, 1, a.shape[-1])
        att = _attn_decode(as_rows(q), as_rows(k), as_rows(v), ck, cv, page_table, l, lam_w, lam_init)
        xs = _merge(xs, rnn_out[None], att.reshape(1, n_dec, -1), gr, ga, mod_s, l,
                    w_pr_b, w_pa_b, w_out_b, tm=n_dec)
        xs = _mlp(xs, mod_s, l, norm2, w_up_b, w_down_b, final_g2, tm=n_dec, final=last)
        k_s.append(k[0]); v_s.append(v[0]); h_s.append(h_new); cv_s.append(jnp.swapaxes(conv_new, 0, 1))

    kshape = lambda n, t: (depth, n, t, N_DH, 2, HEAD_DIM)
    vshape = lambda n, t: (depth, n, t, N_DH, V_DIM)
    k_prompt = jnp.transpose(jnp.stack(k_p).reshape(depth, batch, N_DH, 2, HEAD_DIM, seq),
                             (0, 1, 5, 2, 3, 4))
    return (xp, xs.reshape(n_dec, 1, d),
            k_prompt, jnp.stack(v_p).reshape(vshape(batch, seq)),
            jnp.stack(h_p), jnp.stack(cv_p),
            jnp.stack(k_s).reshape(kshape(n_dec, 1)), jnp.stack(v_s).reshape(vshape(n_dec, 1)),
            jnp.stack(h_s), jnp.stack(cv_s))
```

```python
import functools
import math

import jax
import jax.numpy as jnp
from jax import lax
from jax.experimental import pallas as pl
from jax.experimental.pallas import tpu as pltpu

F32 = jnp.float32
BF16 = jnp.bfloat16

EPS = 1e-6
LRU_C = 8.0
N_RNN_BLOCKS = 8
CONV_WIDTH = 4
N_DH = 4
HEAD_DIM = 64
V_DIM = 2 * HEAD_DIM
N_MAPS = 2 * N_DH
PAGE_SIZE = 128
MASK_VALUE = -1e30
LOG2_E = math.log2(math.e)

V7X_VMEM_BYTES = 64 * 1024 * 1024
VMEM_LIMIT_BYTES = V7X_VMEM_BYTES - 8 * 1024 * 1024
SUBLANES = 8
LANES = 128

ROW_TILE = 512
RNN_CHUNK = 256
ATTN_TILE = 256
ATTN_UNROLL = 4
DECODE_PAGES = 16
ADA_TILE = 1536
FF_CHUNK = 1024


def _params(semantics):
    return pltpu.CompilerParams(dimension_semantics=semantics, vmem_limit_bytes=VMEM_LIMIT_BYTES)


def _rms_norm(x, g):
    return x * lax.rsqrt(jnp.mean(x * x, axis=-1, keepdims=True) + EPS) * g


def _lam_value(lq1_ref, lk1_ref, lq2_ref, lk2_ref, lam_init):
    s1 = jnp.sum(lq1_ref[...] * lk1_ref[...], axis=1, keepdims=True)
    s2 = jnp.sum(lq2_ref[...] * lk2_ref[...], axis=1, keepdims=True)
    return jnp.exp(s1) - jnp.exp(s2) + lam_init


def _ada_kernel(c_ref, w_ref, b_ref, o_ref):
    c = c_ref[...]
    a = (c * jax.nn.sigmoid(c)).astype(BF16)
    o_ref[...] = jnp.dot(a, w_ref[...].astype(BF16), preferred_element_type=F32) + b_ref[...]


def _ada_mod(c_all, w_ada, b_ada):
    depth, d, n = w_ada.shape
    rows = c_all.shape[0]
    return pl.pallas_call(
        _ada_kernel,
        grid=(depth, n // ADA_TILE),
        in_specs=[
            pl.BlockSpec((rows, d), lambda l, j: (0, 0)),
            pl.BlockSpec((None, d, ADA_TILE), lambda l, j: (l, 0, j)),
            pl.BlockSpec((None, 1, ADA_TILE), lambda l, j: (l, 0, j)),
        ],
        out_specs=pl.BlockSpec((None, rows, ADA_TILE), lambda l, j: (l, 0, j)),
        out_shape=jax.ShapeDtypeStruct((depth, rows, n), F32),
        compiler_params=_params(("arbitrary", "arbitrary")),
    )(c_all, w_ada, b_ada.reshape(depth, 1, n))


def _mod_spec(layer, which, rows, d):
    return pl.BlockSpec((None, None, rows, d), lambda g, i: (layer, g, 0, which))


def _stream_columns(d):
    widths = [d, d, N_DH * 2 * HEAD_DIM, N_DH * 2 * HEAD_DIM, N_DH * V_DIM, d, d]
    los = [sum(widths[:i]) for i in range(len(widths))]
    return [(lo, lo + w) for lo, w in zip(los, widths)]


def _in_proj_kernel(x_ref, sh_ref, sc_ref, g_ref, w_ref, *refs, prompt):
    hn = _rms_norm(x_ref[...], g_ref[...]) * (1.0 + sc_ref[...]) + sh_ref[...]
    hb = hn.astype(BF16)
    cols = _stream_columns(x_ref.shape[1])
    stream = lambda idx: jnp.dot(hb, w_ref[:, cols[idx][0]:cols[idx][1]], preferred_element_type=F32)
    q_scale = HEAD_DIM ** -0.5 * (LOG2_E if prompt else 1.0)
    q = stream(2) * q_scale
    k = stream(3)
    v = stream(4)
    if prompt:
        xr_ref, yr_ref, q_ref, k_ref, v_ref, gr_ref, ga_ref, kb_ref, vb_ref = refs
        q_ref[...] = q.T.astype(BF16)
        k_ref[...] = k.T
        kb_ref[...] = k.astype(BF16)
        v_ref[...] = v
        vb_ref[...] = v.T.astype(BF16)
    else:
        xr_ref, yr_ref, q_ref, k_ref, v_ref, gr_ref, ga_ref = refs
        q_ref[...] = q
        k_ref[...] = k
        v_ref[...] = v
    xr_ref[...] = stream(0)
    yr_ref[...] = stream(1).astype(yr_ref.dtype)
    gr_ref[...] = stream(5).astype(gr_ref.dtype)
    ga_ref[...] = stream(6).astype(ga_ref.dtype)


def _in_proj(x, mod, layer, norm_g, w_in, *, tm, prompt):
    groups, rows, d = x.shape
    mod_rows = mod.shape[2]
    widths = [hi - lo for lo, hi in _stream_columns(d)]
    inter = BF16 if prompt else F32
    dtypes = [F32, inter, inter, F32, F32, inter, inter]
    tile = lambda w: pl.BlockSpec((None, tm, w), lambda g, i: (g, i, 0))
    tile_t = lambda w: pl.BlockSpec((None, w, tm), lambda g, i: (g, 0, i))
    out_shapes = [jax.ShapeDtypeStruct((groups, rows, w), t) for w, t in zip(widths, dtypes)]
    out_specs = [tile(w) for w in widths]
    in_specs = [
        tile(d),
        _mod_spec(layer, 0, mod_rows, d),
        _mod_spec(layer, 1, mod_rows, d),
        pl.BlockSpec((None, 1, d), lambda g, i: (layer, 0, 0)),
        pl.BlockSpec((None, d, w_in.shape[2]), lambda g, i: (layer, 0, 0)),
    ]
    args = [x, mod, mod, norm_g, w_in]
    if prompt:
        for idx in (2, 3):
            out_shapes[idx] = jax.ShapeDtypeStruct((groups, widths[idx], rows), dtypes[idx])
            out_specs[idx] = tile_t(widths[idx])
        out_shapes += [jax.ShapeDtypeStruct((groups, rows, widths[3]), BF16),
                       jax.ShapeDtypeStruct((groups, widths[4], rows), BF16)]
        out_specs += [tile(widths[3]), tile_t(widths[4])]
    return pl.pallas_call(
        functools.partial(_in_proj_kernel, prompt=prompt),
        grid=(groups, rows // tm),
        in_specs=in_specs,
        out_specs=out_specs,
        out_shape=out_shapes,
        compiler_params=_params(("parallel", "parallel")),
    )(*args)


def _sigmoid(z):
    return 0.5 * jnp.tanh(0.5 * z) + 0.5


def _softplus(z):
    return jnp.maximum(z, 0.0) + jnp.log1p(jnp.exp(-jnp.abs(z)))


def _block_diag_dot(xb, w_ref):
    blk = xb.shape[1] // N_RNN_BLOCKS
    return jnp.concatenate(
        [jnp.dot(xb[:, n * blk:(n + 1) * blk], w_ref[n], preferred_element_type=F32)
         for n in range(N_RNN_BLOCKS)], axis=1)


def _lru_terms(xc, wa_ref, ba_ref, wx_ref, bx_ref, lam_ref):
    xb = xc.astype(BF16)
    r = _sigmoid(_block_diag_dot(xb, wa_ref) + ba_ref[...])
    i = _sigmoid(_block_diag_dot(xb, wx_ref) + bx_ref[...])
    log_a = -LRU_C * r * _softplus(-lam_ref[...])
    a = jnp.exp(log_a)
    u = -jnp.tanh(log_a) * (a * a + 1.0)
    b = jnp.where(u > 0.0, u * lax.rsqrt(u), 0.0) * (i * xc)
    return a, b


def _gelu_tanh(y):
    return 0.5 * y * (1.0 + jnp.tanh(math.sqrt(2.0 / math.pi) * (y + 0.044715 * (y * y * y))))


def _rnn_prompt_kernel(xr_ref, yr_ref, cw_ref, cb_ref, wa_ref, ba_ref, wx_ref, bx_ref, lam_ref,
                       out_ref, hlast_ref, conv_ref, xbuf, h_scr):
    t = pl.program_id(1)
    steps = xr_ref.shape[0]
    hist = SUBLANES

    @pl.when(t == 0)
    def _():
        xbuf[0:hist, :] = jnp.zeros((hist, xbuf.shape[1]), F32)
        h_scr[...] = jnp.zeros_like(h_scr)

    x = xr_ref[...]
    xbuf[hist:hist + steps, :] = x
    xc = cb_ref[...] + cw_ref[CONV_WIDTH - 1:CONV_WIDTH, :] * x
    for j in range(CONV_WIDTH - 1):
        back = CONV_WIDTH - 1 - j
        xc = xc + cw_ref[j:j + 1, :] * xbuf[hist - back:hist - back + steps, :]
    xbuf[0:hist, :] = x[steps - hist:steps, :]

    a, b = _lru_terms(xc, wa_ref, ba_ref, wx_ref, bx_ref, lam_ref)
    groups = steps // SUBLANES
    a = a.reshape(groups, SUBLANES, a.shape[1])
    b = b.reshape(groups, SUBLANES, b.shape[1])
    sub = lax.broadcasted_iota(jnp.int32, (1, SUBLANES, 1), 1)
    shift = 1
    while shift < SUBLANES:
        a_prev = pltpu.roll(a, shift, axis=1)
        b_prev = pltpu.roll(b, shift, axis=1)
        valid = sub >= shift
        b = jnp.where(valid, a * b_prev + b, b)
        a = jnp.where(valid, a * a_prev, a)
        shift *= 2
    h_prev = h_scr[...]
    hs = []
    for g in range(groups):
        h_g = a[g] * h_prev + b[g]
        hs.append(h_g)
        h_prev = h_g[SUBLANES - 1:SUBLANES, :]
    h = jnp.concatenate(hs, axis=0)
    h_scr[...] = h_prev
    out_ref[...] = (h * _gelu_tanh(yr_ref[...].astype(F32))).astype(out_ref.dtype)

    @pl.when(t == pl.num_programs(1) - 1)
    def _():
        hlast_ref[...] = h[steps - 1:steps, :]
        conv_ref[...] = x[steps - (CONV_WIDTH - 1):steps, :]


def _rnn_weight_specs(layer, c):
    idx2 = lambda *_: (layer, 0, 0)
    idx3 = lambda *_: (layer, 0, 0, 0)
    blk = c // N_RNN_BLOCKS
    return [
        pl.BlockSpec((None, CONV_WIDTH, c), idx2),
        pl.BlockSpec((None, 1, c), idx2),
        pl.BlockSpec((None, N_RNN_BLOCKS, blk, blk), idx3),
        pl.BlockSpec((None, 1, c), idx2),
        pl.BlockSpec((None, N_RNN_BLOCKS, blk, blk), idx3),
        pl.BlockSpec((None, 1, c), idx2),
        pl.BlockSpec((None, 1, c), idx2),
    ]


def _rnn_prompt(xr, yr, layer, rnn_w):
    batch, seq, c = xr.shape
    tile = pl.BlockSpec((None, RNN_CHUNK, c), lambda b, t: (b, t, 0))
    return pl.pallas_call(
        _rnn_prompt_kernel,
        grid=(batch, seq // RNN_CHUNK),
        in_specs=[tile, tile] + _rnn_weight_specs(layer, c),
        out_specs=[
            tile,
            pl.BlockSpec((None, 1, c), lambda b, t: (b, 0, 0)),
            pl.BlockSpec((None, CONV_WIDTH - 1, c), lambda b, t: (b, 0, 0)),
        ],
        out_shape=[
            jax.ShapeDtypeStruct((batch, seq, c), BF16),
            jax.ShapeDtypeStruct((batch, 1, c), F32),
            jax.ShapeDtypeStruct((batch, CONV_WIDTH - 1, c), F32),
        ],
        scratch_shapes=[pltpu.VMEM((SUBLANES + RNN_CHUNK, c), F32), pltpu.VMEM((1, c), F32)],
        compiler_params=_params(("parallel", "arbitrary")),
    )(xr, yr, *rnn_w)


def _rnn_sample_kernel(xr_ref, yr_ref, hist_ref, h0_ref, cw_ref, cb_ref, wa_ref, ba_ref, wx_ref,
                       bx_ref, lam_ref, out_ref, hnew_ref, conv_ref):
    x = xr_ref[...]
    xc = cb_ref[...] + cw_ref[CONV_WIDTH - 1:CONV_WIDTH, :] * x
    for j in range(CONV_WIDTH - 1):
        xc = xc + cw_ref[j:j + 1, :] * hist_ref[j]
    a, b = _lru_terms(xc, wa_ref, ba_ref, wx_ref, bx_ref, lam_ref)
    h = a * h0_ref[...] + b
    hnew_ref[...] = h
    out_ref[...] = h * _gelu_tanh(yr_ref[...])
    for j in range(CONV_WIDTH - 2):
        conv_ref[j] = hist_ref[j + 1]
    conv_ref[CONV_WIDTH - 2] = x


def _rnn_sample(xr, yr, hist_t, h0, layer, rnn_w):
    n, c = xr.shape
    full = pl.BlockSpec((n, c), lambda i: (0, 0))
    hist_spec = pl.BlockSpec((None, CONV_WIDTH - 1, n, c), lambda i: (layer, 0, 0, 0))
    return pl.pallas_call(
        _rnn_sample_kernel,
        grid=(1,),
        in_specs=[full, full, hist_spec, pl.BlockSpec((None, n, c), lambda i: (layer, 0, 0))]
        + _rnn_weight_specs(layer, c),
        out_specs=[full, full, pl.BlockSpec((CONV_WIDTH - 1, n, c), lambda i: (0, 0, 0))],
        out_shape=[
            jax.ShapeDtypeStruct((n, c), F32),
            jax.ShapeDtypeStruct((n, c), F32),
            jax.ShapeDtypeStruct((CONV_WIDTH - 1, n, c), F32),
        ],
        compiler_params=_params(("arbitrary",)),
    )(xr, yr, hist_t, h0, *rnn_w)


def _attn_prompt_kernel(q_ref, k_ref, v_ref, lq1_ref, lk1_ref, lq2_ref, lk2_ref, g_ref, o_ref,
                        m_scr, l_scr, acc_scr, *, lam_init):
    i = pl.program_id(2)
    tq = q_ref.shape[1]
    tk = tq
    qt = q_ref[...]
    feat = lax.broadcasted_iota(jnp.int32, qt.shape, 0)
    zero = jnp.zeros_like(qt)
    qq = jnp.concatenate([jnp.where(feat < HEAD_DIM, qt, zero), jnp.where(feat >= HEAD_DIM, qt, zero)],
                         axis=1)
    m_scr[...] = jnp.full(m_scr.shape, MASK_VALUE, F32)
    l_scr[...] = jnp.zeros_like(l_scr)
    acc_scr[...] = jnp.zeros_like(acc_scr)

    def scores(j):
        start = pl.multiple_of(j * tk, tk)
        return jnp.dot(k_ref[pl.ds(start, tk), :], qq, preferred_element_type=F32)

    def update(j, s, masked):
        start = pl.multiple_of(j * tk, tk)
        vj = v_ref[:, pl.ds(start, tk)]
        if masked:
            kpos = lax.broadcasted_iota(jnp.int32, s.shape, 0)
            c = lax.broadcasted_iota(jnp.int32, s.shape, 1)
            qpos = jnp.where(c >= tq, c - tq, c)
            s = jnp.where(kpos <= qpos, s, MASK_VALUE)
        m_prev = m_scr[...]
        m_next = jnp.maximum(m_prev, jnp.max(s, axis=0, keepdims=True))
        alpha = jnp.exp2(m_prev - m_next)
        p = jnp.exp2(s - m_next)
        l_scr[...] = alpha * l_scr[...] + jnp.sum(p, axis=0, keepdims=True)
        acc_scr[...] = alpha * acc_scr[...] + jnp.dot(vj, p.astype(BF16), preferred_element_type=F32)
        m_scr[...] = m_next

    def group(g, s):
        for u in range(ATTN_UNROLL):
            j = g * ATTN_UNROLL + u
            s_next = scores(j + 1)
            update(j, s, False)
            s = s_next
        return s

    def single(j, s):
        s_next = scores(j + 1)
        update(j, s, False)
        return s_next

    n_groups = i // ATTN_UNROLL
    s = lax.fori_loop(0, n_groups, group, scores(0))
    s = lax.fori_loop(n_groups * ATTN_UNROLL, i, single, s)
    update(i, s, True)

    o = acc_scr[...] / l_scr[...]
    lam = _lam_value(lq1_ref, lk1_ref, lq2_ref, lk2_ref, lam_init)
    o = (o[:, :tq] - lam * o[:, tq:]).T
    o_ref[...] = (_rms_norm(o, g_ref[...]) * (1.0 - lam_init)).astype(o_ref.dtype)


def _lam_specs(layer):
    idx = lambda *_: (layer, 0, 0)
    return [pl.BlockSpec((None, 1, HEAD_DIM), idx)] * 4 + [pl.BlockSpec((None, 1, V_DIM), idx)]


def _attn_prompt(q, k, v, layer, lam_w, lam_init):
    batch, seq, _ = k.shape
    tq = ATTN_TILE
    q_spec = pl.BlockSpec((None, 2 * HEAD_DIM, tq), lambda b, h, i: (b, h, i))
    k_spec = pl.BlockSpec((None, seq, 2 * HEAD_DIM), lambda b, h, i: (b, 0, h))
    v_spec = pl.BlockSpec((None, V_DIM, seq), lambda b, h, i: (b, h, 0))
    return pl.pallas_call(
        functools.partial(_attn_prompt_kernel, lam_init=lam_init),
        grid=(batch, N_DH, seq // tq),
        in_specs=[q_spec, k_spec, v_spec] + _lam_specs(layer),
        out_specs=pl.BlockSpec((None, tq, V_DIM), lambda b, h, i: (b, i, h)),
        out_shape=jax.ShapeDtypeStruct((batch, seq, N_DH * V_DIM), BF16),
        scratch_shapes=[pltpu.VMEM((1, 2 * tq), F32), pltpu.VMEM((1, 2 * tq), F32),
                        pltpu.VMEM((V_DIM, 2 * tq), F32)],
        compiler_params=_params(("parallel", "parallel", "arbitrary")),
    )(q, k, v, *lam_w)


def _attn_decode_kernel(pt_ref, q_ref, kn_ref, vn_ref, lq1_ref, lk1_ref, lq2_ref, lk2_ref, g_ref,
                        *refs, lam_init, pages):
    k_refs = refs[:pages]
    v_refs = refs[pages:2 * pages]
    o_ref = refs[2 * pages]
    m_scr, l_scr, acc_scr = refs[2 * pages + 1:]
    c = pl.program_id(1)
    width = q_ref.shape[1]

    @pl.when(c == 0)
    def _():
        m_scr[...] = jnp.full(m_scr.shape, MASK_VALUE, F32)
        l_scr[...] = jnp.zeros_like(l_scr)
        acc_scr[...] = jnp.zeros_like(acc_scr)

    map_row = lax.broadcasted_iota(jnp.int32, (N_MAPS, width), 0)
    lane = lax.broadcasted_iota(jnp.int32, (N_MAPS, width), 1)
    qbd = jnp.where(lane // HEAD_DIM == map_row, jnp.broadcast_to(q_ref[...], (N_MAPS, width)), 0.0)
    qb = qbd.astype(BF16)

    s = jnp.concatenate(
        [jnp.dot(qb, k_refs[i][...].astype(BF16), preferred_element_type=F32) for i in range(pages)],
        axis=1)
    m_prev = m_scr[...]
    m_next = jnp.maximum(m_prev, jnp.max(s, axis=1, keepdims=True))
    alpha = jnp.exp(m_prev - m_next)
    p = jnp.exp(s - pltpu.repeat(m_next, pages, axis=1))
    l_scr[...] = alpha * l_scr[...] + jnp.sum(p, axis=1, keepdims=True)
    pb = p.astype(BF16)
    heads = []
    for h in range(N_DH):
        pv = None
        for i in range(pages):
            v_h = v_refs[i][pl.ds(h, PAGE_SIZE, stride=N_DH), :].astype(BF16)
            part = jnp.dot(pb[:, i * PAGE_SIZE:(i + 1) * PAGE_SIZE], v_h, preferred_element_type=F32)
            pv = part if pv is None else pv + part
        heads.append(pv)
    acc_scr[...] = (pltpu.repeat(alpha, width // LANES, axis=1) * acc_scr[...]
                    + jnp.concatenate(heads, axis=1))
    m_scr[...] = m_next

    @pl.when(c == pl.num_programs(1) - 1)
    def _():
        s_new = jnp.sum(qbd * kn_ref[...], axis=1, keepdims=True)
        m_prev = m_scr[...]
        m_fin = jnp.maximum(m_prev, s_new)
        alpha = jnp.exp(m_prev - m_fin)
        p_new = jnp.exp(s_new - m_fin)
        l_fin = alpha * l_scr[...] + p_new
        acc = (pltpu.repeat(alpha, width // LANES, axis=1) * acc_scr[...]
               + pltpu.repeat(p_new, width // LANES, axis=1) * vn_ref[...])
        res = acc / pltpu.repeat(l_fin, width // LANES, axis=1)
        head = lane // V_DIM
        first = jnp.sum(jnp.where(map_row == 2 * head, res, 0.0), axis=0, keepdims=True)
        second = jnp.sum(jnp.where(map_row == 2 * head + 1, res, 0.0), axis=0, keepdims=True)
        lam = _lam_value(lq1_ref, lk1_ref, lq2_ref, lk2_ref, lam_init)
        o = first - lam * second
        g = g_ref[...]
        o_ref[...] = jnp.concatenate(
            [_rms_norm(o[:, h * V_DIM:(h + 1) * V_DIM], g) for h in range(N_DH)],
            axis=1) * (1.0 - lam_init)


def _attn_decode(q, k_new, v_new, cache_k, cache_v, page_table, layer, lam_w, lam_init):
    n, _, width = q.shape
    n_pages = page_table.shape[1]
    pages = DECODE_PAGES
    row = pl.BlockSpec((None, 1, width), lambda b, c, pt: (b, 0, 0))
    lam_specs = [pl.BlockSpec((None, 1, HEAD_DIM), lambda b, c, pt: (layer, 0, 0))] * 4
    lam_specs.append(pl.BlockSpec((None, 1, V_DIM), lambda b, c, pt: (layer, 0, 0)))

    def page_spec(i, shape):
        return pl.BlockSpec((None, None) + shape,
                            lambda b, c, pt: (layer, pt[b * n_pages + c * pages + i], 0, 0))

    k_pages = [page_spec(i, (width, PAGE_SIZE)) for i in range(pages)]
    v_pages = [page_spec(i, (PAGE_SIZE * N_DH, V_DIM)) for i in range(pages)]

    grid_spec = pltpu.PrefetchScalarGridSpec(
        num_scalar_prefetch=1,
        grid=(n, n_pages // pages),
        in_specs=[row, row, row] + lam_specs + k_pages + v_pages,
        out_specs=row,
        scratch_shapes=[pltpu.VMEM((N_MAPS, LANES), F32), pltpu.VMEM((N_MAPS, LANES), F32),
                        pltpu.VMEM((N_MAPS, width), F32)],
    )
    return pl.pallas_call(
        functools.partial(_attn_decode_kernel, lam_init=lam_init, pages=pages),
        grid_spec=grid_spec,
        out_shape=jax.ShapeDtypeStruct((n, 1, width), F32),
        compiler_params=_params(("parallel", "arbitrary")),
    )(page_table.reshape(-1), q, k_new, v_new, *lam_w, *([cache_k] * pages), *([cache_v] * pages))


def _merge_kernel(x_ref, rnn_ref, att_ref, gr_ref, ga_ref, g1_ref, wpr_ref, wpa_ref, wout_ref, o_ref):
    pr = jnp.dot(rnn_ref[...].astype(BF16), wpr_ref[...], preferred_element_type=F32)
    pa = jnp.dot(att_ref[...].astype(BF16), wpa_ref[...], preferred_element_type=F32)
    merged = (jax.nn.sigmoid(gr_ref[...].astype(F32)) * pr
              + jax.nn.sigmoid(ga_ref[...].astype(F32)) * pa)
    out = jnp.dot(merged.astype(BF16), wout_ref[...], preferred_element_type=F32)
    o_ref[...] = x_ref[...] + g1_ref[...] * out


def _merge(x, rnn_out, att, gr, ga, mod, layer, w_pr, w_pa, w_out, *, tm):
    groups, rows, d = x.shape
    tile = lambda a: pl.BlockSpec((None, tm, a.shape[2]), lambda g, i: (g, i, 0))
    weight = lambda w: pl.BlockSpec((None,) + w.shape[1:], lambda g, i: (layer, 0, 0))
    return pl.pallas_call(
        _merge_kernel,
        grid=(groups, rows // tm),
        in_specs=[tile(x), tile(rnn_out), tile(att), tile(gr), tile(ga),
                  _mod_spec(layer, 2, mod.shape[2], d), weight(w_pr), weight(w_pa), weight(w_out)],
        out_specs=tile(x),
        out_shape=jax.ShapeDtypeStruct(x.shape, F32),
        compiler_params=_params(("parallel", "parallel")),
    )(x, rnn_out, att, gr, ga, mod, w_pr, w_pa, w_out)


def _mlp_kernel(x_ref, sh_ref, sc_ref, g2_ref, ng_ref, wup_ref, wdn_ref, *rest, final):
    x = x_ref[...]
    hb = (_rms_norm(x, ng_ref[...]) * (1.0 + sc_ref[...]) + sh_ref[...]).astype(BF16)
    d_ff = wup_ref.shape[1]
    acc = jnp.zeros(x.shape, F32)
    for lo in range(0, d_ff, FF_CHUNK):
        u = jnp.maximum(jnp.dot(hb, wup_ref[:, lo:lo + FF_CHUNK], preferred_element_type=F32), 0.0)
        acc = acc + jnp.dot((u * u).astype(BF16), wdn_ref[lo:lo + FF_CHUNK, :],
                            preferred_element_type=F32)
    y = x + g2_ref[...] * acc
    if final:
        fg_ref, o_ref = rest
        o_ref[...] = _rms_norm(y, fg_ref[...])
    else:
        (o_ref,) = rest
        o_ref[...] = y


def _mlp(x, mod, layer, norm_g, w_up, w_down, final_g, *, tm, final):
    groups, rows, d = x.shape
    tile = pl.BlockSpec((None, tm, d), lambda g, i: (g, i, 0))
    weight = lambda w: pl.BlockSpec((None,) + w.shape[1:], lambda g, i: (layer, 0, 0))
    mod_rows = mod.shape[2]
    in_specs = [tile, _mod_spec(layer, 3, mod_rows, d), _mod_spec(layer, 4, mod_rows, d),
                _mod_spec(layer, 5, mod_rows, d), pl.BlockSpec((None, 1, d), lambda g, i: (layer, 0, 0)),
                weight(w_up), weight(w_down)]
    args = [x, mod, mod, mod, norm_g, w_up, w_down]
    if final:
        in_specs.append(pl.BlockSpec((1, d), lambda g, i: (0, 0)))
        args.append(final_g)
    return pl.pallas_call(
        functools.partial(_mlp_kernel, final=final),
        grid=(groups, rows // tm),
        in_specs=in_specs,
        out_specs=tile,
        out_shape=jax.ShapeDtypeStruct(x.shape, F32),
        compiler_params=_params(("parallel", "parallel")),
    )(*args)


def kernel(x_prompt, x_sample, c_prompt, c_sample, cache_k, cache_v, state_rnn, state_conv, page_table, w_ada, b_ada, norm1_g, norm2_g, w_in, conv_w, conv_b, w_a, b_a, w_x, b_x, lru_lambda, lq1, lk1, lq2, lk2, subln_g, w_pr, w_pa, w_out, w_up, w_down, final_g):
    depth, d = norm1_g.shape
    batch, seq, _ = x_prompt.shape
    n_dec = x_sample.shape[0]
    d_att = N_DH * 2 * HEAD_DIM
    assert x_sample.shape[1] == 1, "one new token per running sequence"
    assert seq % ROW_TILE == 0 and seq % RNN_CHUNK == 0 and seq % ATTN_TILE == 0
    assert page_table.shape[1] % DECODE_PAGES == 0 and cache_k.shape[2] == PAGE_SIZE

    w_in_b, w_pr_b, w_pa_b, w_out_b, w_up_b, w_down_b, w_a_b, w_x_b = (
        w.astype(BF16) for w in (w_in, w_pr, w_pa, w_out, w_up, w_down, w_a, w_x))
    row3 = lambda a: a.reshape(depth, 1, a.shape[-1])
    rnn_w = (conv_w, row3(conv_b), w_a_b, row3(b_a), w_x_b, row3(b_x), row3(lru_lambda))
    lam_w = (row3(lq1), row3(lk1), row3(lq2), row3(lk2), row3(subln_g))
    norm1, norm2 = row3(norm1_g), row3(norm2_g)
    final_g2 = final_g.reshape(1, d)

    n_cond = n_dec + batch
    pad = -n_cond % (2 * SUBLANES)
    c_all = jnp.concatenate([c_sample, c_prompt, jnp.zeros((pad, d), F32)], axis=0)
    mods = _ada_mod(c_all, w_ada, b_ada)
    mod_s = mods[:, :n_dec].reshape(depth, 1, n_dec, 6 * d)
    mod_p = mods[:, n_dec:n_cond].reshape(depth, batch, 1, 6 * d)

    ck = jnp.transpose(cache_k, (0, 1, 3, 4, 5, 2)).reshape(depth, cache_k.shape[1], d_att, PAGE_SIZE)
    cv = cache_v.reshape(depth, cache_v.shape[1], PAGE_SIZE * N_DH, V_DIM)
    hist_t = jnp.swapaxes(state_conv, 1, 2)

    xp = x_prompt
    xs = x_sample.reshape(1, n_dec, d)
    k_p, v_p, h_p, cv_p, k_s, v_s, h_s, cv_s = ([] for _ in range(8))
    for l in range(depth):
        lam_init = 0.8 - 0.6 * math.exp(-0.3 * l)
        last = l == depth - 1

        xr, yr, qt, kt, v, gr, ga, kb, vtb = _in_proj(xp, mod_p, l, norm1, w_in_b, tm=ROW_TILE, prompt=True)
        rnn_out, h_last, conv_last = _rnn_prompt(xr, yr, l, rnn_w)
        att = _attn_prompt(qt, kb, vtb, l, lam_w, lam_init)
        xp = _merge(xp, rnn_out, att, gr, ga, mod_p, l, w_pr_b, w_pa_b, w_out_b, tm=ROW_TILE)
        xp = _mlp(xp, mod_p, l, norm2, w_up_b, w_down_b, final_g2, tm=ROW_TILE, final=last)
        k_p.append(kt); v_p.append(v); h_p.append(h_last[:, 0]); cv_p.append(conv_last)

        xr, yr, q, k, v, gr, ga = _in_proj(xs, mod_s, l, norm1, w_in_b, tm=n_dec, prompt=False)
        rnn_out, h_new, conv_new = _rnn_sample(xr[0], yr[0], hist_t, state_rnn, l, rnn_w)
        as_rows = lambda a: a.reshape(n_dec, 1, a.shape[-1])
        att = _attn_decode(as_rows(q), as_rows(k), as_rows(v), ck, cv, page_table, l, lam_w, lam_init)
        xs = _merge(xs, rnn_out[None], att.reshape(1, n_dec, -1), gr, ga, mod_s, l,
                    w_pr_b, w_pa_b, w_out_b, tm=n_dec)
        xs = _mlp(xs, mod_s, l, norm2, w_up_b, w_down_b, final_g2, tm=n_dec, final=last)
        k_s.append(k[0]); v_s.append(v[0]); h_s.append(h_new); cv_s.append(jnp.swapaxes(conv_new, 0, 1))

    kshape = lambda n, t: (depth, n, t, N_DH, 2, HEAD_DIM)
    vshape = lambda n, t: (depth, n, t, N_DH, V_DIM)
    k_prompt = jnp.transpose(jnp.stack(k_p).reshape(depth, batch, N_DH, 2, HEAD_DIM, seq),
                             (0, 1, 5, 2, 3, 4))
    return (xp, xs.reshape(n_dec, 1, d),
            k_prompt, jnp.stack(v_p).reshape(vshape(batch, seq)),
            jnp.stack(h_p), jnp.stack(cv_p),
            jnp.stack(k_s).reshape(kshape(n_dec, 1)), jnp.stack(v_s).reshape(vshape(n_dec, 1)),
            jnp.stack(h_s), jnp.stack(cv_s))
```

```python
import functools
import math

import jax
import jax.numpy as jnp
from jax import lax
from jax.experimental import pallas as pl
from jax.experimental.pallas import tpu as pltpu

F32 = jnp.float32
BF16 = jnp.bfloat16

EPS = 1e-6
LRU_C = 8.0
N_RNN_BLOCKS = 8
CONV_WIDTH = 4
N_DH = 4
HEAD_DIM = 64
V_DIM = 2 * HEAD_DIM
N_MAPS = 2 * N_DH
PAGE_SIZE = 128
MASK_VALUE = -1e30
LOG2_E = math.log2(math.e)

V7X_VMEM_BYTES = 64 * 1024 * 1024
VMEM_LIMIT_BYTES = V7X_VMEM_BYTES - 8 * 1024 * 1024
SUBLANES = 8
LANES = 128

ROW_TILE = 512
RNN_CHUNK = 256
ATTN_TILE = 256
ATTN_UNROLL = 4
ATTN_HEADS = 4
DECODE_PAGES = 16
ADA_TILE = 1536
FF_CHUNK = 1024


def _params(semantics):
    return pltpu.CompilerParams(dimension_semantics=semantics, vmem_limit_bytes=VMEM_LIMIT_BYTES)


def _repeat(x, n, axis):
    return jnp.concatenate([x] * n, axis=axis)


def _rms_norm(x, g):
    return x * lax.rsqrt(jnp.mean(x * x, axis=-1, keepdims=True) + EPS) * g


def _lam_value(lq1_ref, lk1_ref, lq2_ref, lk2_ref, lam_init):
    s1 = jnp.sum(lq1_ref[...] * lk1_ref[...], axis=1, keepdims=True)
    s2 = jnp.sum(lq2_ref[...] * lk2_ref[...], axis=1, keepdims=True)
    return jnp.exp(s1) - jnp.exp(s2) + lam_init


def _ada_kernel(c_ref, w_ref, b_ref, o_ref):
    c = c_ref[...]
    a = (c * jax.nn.sigmoid(c)).astype(BF16)
    o_ref[...] = jnp.dot(a, w_ref[...].astype(BF16), preferred_element_type=F32) + b_ref[...]


def _ada_mod(c_all, w_ada, b_ada):
    depth, d, n = w_ada.shape
    rows = c_all.shape[0]
    return pl.pallas_call(
        _ada_kernel,
        grid=(depth, n // ADA_TILE),
        in_specs=[
            pl.BlockSpec((rows, d), lambda l, j: (0, 0)),
            pl.BlockSpec((None, d, ADA_TILE), lambda l, j: (l, 0, j)),
            pl.BlockSpec((None, 1, ADA_TILE), lambda l, j: (l, 0, j)),
        ],
        out_specs=pl.BlockSpec((None, rows, ADA_TILE), lambda l, j: (l, 0, j)),
        out_shape=jax.ShapeDtypeStruct((depth, rows, n), F32),
        compiler_params=_params(("arbitrary", "arbitrary")),
    )(c_all, w_ada, b_ada.reshape(depth, 1, n))


def _mod_spec(layer, which, rows, d):
    return pl.BlockSpec((None, None, rows, d), lambda g, i: (layer, g, 0, which))


def _stream_columns(d):
    widths = [d, d, N_DH * 2 * HEAD_DIM, N_DH * 2 * HEAD_DIM, N_DH * V_DIM, d, d]
    los = [sum(widths[:i]) for i in range(len(widths))]
    return [(lo, lo + w) for lo, w in zip(los, widths)]


def _in_proj_kernel(x_ref, sh_ref, sc_ref, g_ref, w_ref, *refs, prompt):
    hn = _rms_norm(x_ref[...], g_ref[...]) * (1.0 + sc_ref[...]) + sh_ref[...]
    hb = hn.astype(BF16)
    cols = _stream_columns(x_ref.shape[1])
    stream = lambda idx: jnp.dot(hb, w_ref[:, cols[idx][0]:cols[idx][1]], preferred_element_type=F32)
    q_scale = HEAD_DIM ** -0.5 * (LOG2_E if prompt else 1.0)
    q = stream(2) * q_scale
    k = stream(3)
    v = stream(4)
    if prompt:
        xr_ref, yr_ref, q_ref, k_ref, v_ref, gr_ref, ga_ref, kb_ref, vb_ref = refs[-9:]
        q_ref[...] = q.T.astype(BF16)
        k_ref[...] = k.T
        kb_ref[...] = k.astype(BF16)
        for h in range(N_DH):
            v_ref[pl.ds(h, v.shape[0], stride=N_DH), :] = v[:, h * V_DIM:(h + 1) * V_DIM]
        vb_ref[...] = v.T.astype(BF16)
    else:
        xr_ref, yr_ref, q_ref, k_ref, v_ref, gr_ref, ga_ref = refs
        q_ref[...] = q
        k_ref[...] = k
        v_ref[...] = v
    xr_ref[...] = stream(0)
    yr_ref[...] = stream(1).astype(yr_ref.dtype)
    gr_ref[...] = stream(5).astype(gr_ref.dtype)
    ga_ref[...] = stream(6).astype(ga_ref.dtype)


def _in_proj(x, mod, layer, norm_g, w_in, *, tm, prompt, stacked=None):
    groups, rows, d = x.shape
    mod_rows = mod.shape[2]
    widths = [hi - lo for lo, hi in _stream_columns(d)]
    inter = BF16 if prompt else F32
    dtypes = [F32, inter, inter, F32, F32, inter, inter]
    tile = lambda w: pl.BlockSpec((None, tm, w), lambda g, i: (g, i, 0))
    tile_t = lambda w: pl.BlockSpec((None, w, tm), lambda g, i: (g, 0, i))
    out_shapes = [jax.ShapeDtypeStruct((groups, rows, w), t) for w, t in zip(widths, dtypes)]
    out_specs = [tile(w) for w in widths]
    in_specs = [
        tile(d),
        _mod_spec(layer, 0, mod_rows, d),
        _mod_spec(layer, 1, mod_rows, d),
        pl.BlockSpec((None, 1, d), lambda g, i: (layer, 0, 0)),
        pl.BlockSpec((None, d, w_in.shape[2]), lambda g, i: (layer, 0, 0)),
    ]
    args = [x, mod, mod, norm_g, w_in]
    aliases = {}
    if prompt:
        depth = w_in.shape[0]
        out_shapes[2] = jax.ShapeDtypeStruct((groups, widths[2], rows), BF16)
        out_specs[2] = tile_t(widths[2])
        out_shapes[3] = jax.ShapeDtypeStruct((depth, groups, widths[3], rows), F32)
        out_specs[3] = pl.BlockSpec((None, None, widths[3], tm), lambda g, i: (layer, g, 0, i))
        out_shapes[4] = jax.ShapeDtypeStruct((depth, groups, rows * N_DH, V_DIM), F32)
        out_specs[4] = pl.BlockSpec((None, None, tm * N_DH, V_DIM), lambda g, i: (layer, g, i, 0))
        out_shapes += [jax.ShapeDtypeStruct((groups, rows, widths[3]), BF16),
                       jax.ShapeDtypeStruct((groups, widths[4], rows), BF16)]
        out_specs += [tile(widths[3]), tile_t(widths[4])]
        if stacked is not None:
            aliases = {len(args): 3, len(args) + 1: 4}
            in_specs += [pl.BlockSpec(memory_space=pl.ANY)] * 2
            args += list(stacked)
    return pl.pallas_call(
        functools.partial(_in_proj_kernel, prompt=prompt),
        grid=(groups, rows // tm),
        in_specs=in_specs,
        out_specs=out_specs,
        out_shape=out_shapes,
        input_output_aliases=aliases,
        compiler_params=_params(("parallel", "parallel")),
    )(*args)


def _sigmoid(z):
    return 0.5 * jnp.tanh(0.5 * z) + 0.5


def _softplus(z):
    return jnp.maximum(z, 0.0) + jnp.log1p(jnp.exp(-jnp.abs(z)))


def _block_diag_dot(xb, w_ref):
    blk = xb.shape[1] // N_RNN_BLOCKS
    return jnp.concatenate(
        [jnp.dot(xb[:, n * blk:(n + 1) * blk], w_ref[n], preferred_element_type=F32)
         for n in range(N_RNN_BLOCKS)], axis=1)


def _lru_terms(xc, wa_ref, ba_ref, wx_ref, bx_ref, lam_ref):
    xb = xc.astype(BF16)
    r = _sigmoid(_block_diag_dot(xb, wa_ref) + ba_ref[...])
    i = _sigmoid(_block_diag_dot(xb, wx_ref) + bx_ref[...])
    log_a = -LRU_C * r * _softplus(-lam_ref[...])
    a = jnp.exp(log_a)
    u = -jnp.tanh(log_a) * (a * a + 1.0)
    b = jnp.where(u > 0.0, u * lax.rsqrt(u), 0.0) * (i * xc)
    return a, b


def _gelu_tanh(y):
    return 0.5 * y * (1.0 + jnp.tanh(math.sqrt(2.0 / math.pi) * (y + 0.044715 * (y * y * y))))


def _rnn_prompt_kernel(xr_ref, yr_ref, cw_ref, cb_ref, wa_ref, ba_ref, wx_ref, bx_ref, lam_ref,
                       out_ref, hlast_ref, conv_ref, xbuf, h_scr):
    t = pl.program_id(1)
    steps = xr_ref.shape[0]
    hist = SUBLANES

    @pl.when(t == 0)
    def _():
        xbuf[0:hist, :] = jnp.zeros((hist, xbuf.shape[1]), F32)
        h_scr[...] = jnp.zeros_like(h_scr)

    x = xr_ref[...]
    xbuf[hist:hist + steps, :] = x
    xc = cb_ref[...] + cw_ref[CONV_WIDTH - 1:CONV_WIDTH, :] * x
    for j in range(CONV_WIDTH - 1):
        back = CONV_WIDTH - 1 - j
        xc = xc + cw_ref[j:j + 1, :] * xbuf[hist - back:hist - back + steps, :]
    xbuf[0:hist, :] = x[steps - hist:steps, :]

    a, b = _lru_terms(xc, wa_ref, ba_ref, wx_ref, bx_ref, lam_ref)
    groups = steps // SUBLANES
    a = a.reshape(groups, SUBLANES, a.shape[1])
    b = b.reshape(groups, SUBLANES, b.shape[1])
    sub = lax.broadcasted_iota(jnp.int32, (1, SUBLANES, 1), 1)
    shift = 1
    while shift < SUBLANES:
        a_prev = pltpu.roll(a, shift, axis=1)
        b_prev = pltpu.roll(b, shift, axis=1)
        valid = sub >= shift
        b = jnp.where(valid, a * b_prev + b, b)
        a = jnp.where(valid, a * a_prev, a)
        shift *= 2
    h_prev = h_scr[...]
    hs = []
    for g in range(groups):
        h_g = a[g] * h_prev + b[g]
        hs.append(h_g)
        h_prev = h_g[SUBLANES - 1:SUBLANES, :]
    h = jnp.concatenate(hs, axis=0)
    h_scr[...] = h_prev
    out_ref[...] = (h * _gelu_tanh(yr_ref[...].astype(F32))).astype(out_ref.dtype)

    @pl.when(t == pl.num_programs(1) - 1)
    def _():
        hlast_ref[...] = h[steps - 1:steps, :]
        conv_ref[...] = x[steps - (CONV_WIDTH - 1):steps, :]


def _rnn_weight_specs(layer, c):
    idx2 = lambda *_: (layer, 0, 0)
    idx3 = lambda *_: (layer, 0, 0, 0)
    blk = c // N_RNN_BLOCKS
    return [
        pl.BlockSpec((None, CONV_WIDTH, c), idx2),
        pl.BlockSpec((None, 1, c), idx2),
        pl.BlockSpec((None, N_RNN_BLOCKS, blk, blk), idx3),
        pl.BlockSpec((None, 1, c), idx2),
        pl.BlockSpec((None, N_RNN_BLOCKS, blk, blk), idx3),
        pl.BlockSpec((None, 1, c), idx2),
        pl.BlockSpec((None, 1, c), idx2),
    ]


def _rnn_prompt(xr, yr, layer, rnn_w):
    batch, seq, c = xr.shape
    tile = pl.BlockSpec((None, RNN_CHUNK, c), lambda b, t: (b, t, 0))
    return pl.pallas_call(
        _rnn_prompt_kernel,
        grid=(batch, seq // RNN_CHUNK),
        in_specs=[tile, tile] + _rnn_weight_specs(layer, c),
        out_specs=[
            tile,
            pl.BlockSpec((None, 1, c), lambda b, t: (b, 0, 0)),
            pl.BlockSpec((None, CONV_WIDTH - 1, c), lambda b, t: (b, 0, 0)),
        ],
        out_shape=[
            jax.ShapeDtypeStruct((batch, seq, c), BF16),
            jax.ShapeDtypeStruct((batch, 1, c), F32),
            jax.ShapeDtypeStruct((batch, CONV_WIDTH - 1, c), F32),
        ],
        scratch_shapes=[pltpu.VMEM((SUBLANES + RNN_CHUNK, c), F32), pltpu.VMEM((1, c), F32)],
        compiler_params=_params(("parallel", "arbitrary")),
    )(xr, yr, *rnn_w)


def _rnn_sample_kernel(xr_ref, yr_ref, hist_ref, h0_ref, cw_ref, cb_ref, wa_ref, ba_ref, wx_ref,
                       bx_ref, lam_ref, out_ref, hnew_ref, conv_ref):
    x = xr_ref[...]
    xc = cb_ref[...] + cw_ref[CONV_WIDTH - 1:CONV_WIDTH, :] * x
    for j in range(CONV_WIDTH - 1):
        xc = xc + cw_ref[j:j + 1, :] * hist_ref[j]
    a, b = _lru_terms(xc, wa_ref, ba_ref, wx_ref, bx_ref, lam_ref)
    h = a * h0_ref[...] + b
    hnew_ref[...] = h
    out_ref[...] = h * _gelu_tanh(yr_ref[...])
    for j in range(CONV_WIDTH - 2):
        conv_ref[j] = hist_ref[j + 1]
    conv_ref[CONV_WIDTH - 2] = x


def _rnn_sample(xr, yr, hist_t, h0, layer, rnn_w):
    n, c = xr.shape
    full = pl.BlockSpec((n, c), lambda i: (0, 0))
    hist_spec = pl.BlockSpec((None, CONV_WIDTH - 1, n, c), lambda i: (layer, 0, 0, 0))
    return pl.pallas_call(
        _rnn_sample_kernel,
        grid=(1,),
        in_specs=[full, full, hist_spec, pl.BlockSpec((None, n, c), lambda i: (layer, 0, 0))]
        + _rnn_weight_specs(layer, c),
        out_specs=[full, full, pl.BlockSpec((CONV_WIDTH - 1, n, c), lambda i: (0, 0, 0))],
        out_shape=[
            jax.ShapeDtypeStruct((n, c), F32),
            jax.ShapeDtypeStruct((n, c), F32),
            jax.ShapeDtypeStruct((CONV_WIDTH - 1, n, c), F32),
        ],
        compiler_params=_params(("arbitrary",)),
    )(xr, yr, hist_t, h0, *rnn_w)


def _attn_prompt_kernel(q_ref, k_ref, v_ref, lq1_ref, lk1_ref, lq2_ref, lk2_ref, g_ref, o_ref,
                        m_scr, l_scr, acc_scr, s_scr, *, lam_init):
    i = pl.program_id(2)
    tq = q_ref.shape[1]
    tk = tq
    feats = 2 * HEAD_DIM
    heads = q_ref.shape[0] // feats
    feat = lax.broadcasted_iota(jnp.int32, (feats, tq), 0)
    qq = []
    for h in range(heads):
        qt = q_ref[h * feats:(h + 1) * feats, :]
        zero = jnp.zeros_like(qt)
        qq.append(jnp.concatenate([jnp.where(feat < HEAD_DIM, qt, zero),
                                   jnp.where(feat >= HEAD_DIM, qt, zero)], axis=1))
    m_scr[...] = jnp.full(m_scr.shape, MASK_VALUE, F32)
    l_scr[...] = jnp.zeros_like(l_scr)
    acc_scr[...] = jnp.zeros_like(acc_scr)

    def scores(h, j, slot):
        start = pl.multiple_of(j * tk, tk)
        kj = k_ref[pl.ds(start, tk), h * feats:(h + 1) * feats]
        s_scr[h, slot] = jnp.dot(kj, qq[h], preferred_element_type=F32)

    def update(h, j, slot, masked):
        start = pl.multiple_of(j * tk, tk)
        vj = v_ref[h * V_DIM:(h + 1) * V_DIM, pl.ds(start, tk)]
        s = s_scr[h, slot]
        if masked:
            kpos = lax.broadcasted_iota(jnp.int32, s.shape, 0)
            c = lax.broadcasted_iota(jnp.int32, s.shape, 1)
            qpos = jnp.where(c >= tq, c - tq, c)
            s = jnp.where(kpos <= qpos, s, MASK_VALUE)
        m_prev = m_scr[h]
        m_next = jnp.maximum(m_prev, jnp.max(s, axis=0, keepdims=True))
        alpha = jnp.exp2(m_prev - m_next)
        p = jnp.exp2(s - m_next)
        l_scr[h] = alpha * l_scr[h] + jnp.sum(p, axis=0, keepdims=True)
        acc_scr[h] = alpha * acc_scr[h] + jnp.dot(vj, p.astype(BF16), preferred_element_type=F32)
        m_scr[h] = m_next

    def advance(j, slot, masked):
        for h in range(heads):
            if not masked:
                scores(h, j + 1, 1 - slot)
            other = (h + 1) % heads
            update(other, j, slot, masked)

    def group(g, carry):
        for u in range(ATTN_UNROLL):
            advance(g * ATTN_UNROLL + u, u % 2, False)
        return carry

    for h in range(heads):
        scores(h, 0, 0)
    n_groups = i // ATTN_UNROLL
    lax.fori_loop(0, n_groups, group, 0)
    done = n_groups * ATTN_UNROLL
    width = ATTN_UNROLL // 2
    while width >= 2:
        @pl.when((i & width) != 0)
        def _(done=done, width=width):
            for u in range(width):
                advance(done + u, u % 2, False)
        done = done + (i & width)
        width //= 2

    @pl.when((i & 1) != 0)
    def _():
        advance(done, 0, False)
        advance(done + 1, 1, True)

    @pl.when((i & 1) == 0)
    def _():
        advance(done, 0, True)

    lam = _lam_value(lq1_ref, lk1_ref, lq2_ref, lk2_ref, lam_init)
    for h in range(heads):
        o = acc_scr[h] / l_scr[h]
        o = (o[:, :tq] - lam * o[:, tq:]).T
        o_ref[:, h * V_DIM:(h + 1) * V_DIM] = (
            _rms_norm(o, g_ref[...]) * (1.0 - lam_init)).astype(o_ref.dtype)


def _lam_specs(layer):
    idx = lambda *_: (layer, 0, 0)
    return [pl.BlockSpec((None, 1, HEAD_DIM), idx)] * 4 + [pl.BlockSpec((None, 1, V_DIM), idx)]


def _attn_prompt(q, k, v, layer, lam_w, lam_init):
    batch, seq, _ = k.shape
    tq = ATTN_TILE
    hs = ATTN_HEADS
    q_spec = pl.BlockSpec((None, hs * 2 * HEAD_DIM, tq), lambda b, h, i: (b, h, i))
    k_spec = pl.BlockSpec((None, seq, hs * 2 * HEAD_DIM), lambda b, h, i: (b, 0, h))
    v_spec = pl.BlockSpec((None, hs * V_DIM, seq), lambda b, h, i: (b, h, 0))
    return pl.pallas_call(
        functools.partial(_attn_prompt_kernel, lam_init=lam_init),
        grid=(batch, N_DH // hs, seq // tq),
        in_specs=[q_spec, k_spec, v_spec] + _lam_specs(layer),
        out_specs=pl.BlockSpec((None, tq, hs * V_DIM), lambda b, h, i: (b, i, h)),
        out_shape=jax.ShapeDtypeStruct((batch, seq, N_DH * V_DIM), BF16),
        scratch_shapes=[pltpu.VMEM((hs, 1, 2 * tq), F32), pltpu.VMEM((hs, 1, 2 * tq), F32),
                        pltpu.VMEM((hs, V_DIM, 2 * tq), F32), pltpu.VMEM((hs, 2, tq, 2 * tq), F32)],
        compiler_params=_params(("parallel", "parallel", "arbitrary")),
    )(q, k, v, *lam_w)


def _attn_decode_kernel(pt_ref, q_ref, kn_ref, vn_ref, lq1_ref, lk1_ref, lq2_ref, lk2_ref, g_ref,
                        *refs, lam_init, pages):
    k_refs = refs[:pages]
    v_refs = refs[pages:2 * pages]
    o_ref = refs[2 * pages]
    m_scr, l_scr, acc_scr = refs[2 * pages + 1:]
    c = pl.program_id(1)
    width = q_ref.shape[1]

    @pl.when(c == 0)
    def _():
        m_scr[...] = jnp.full(m_scr.shape, MASK_VALUE, F32)
        l_scr[...] = jnp.zeros_like(l_scr)
        acc_scr[...] = jnp.zeros_like(acc_scr)

    map_row = lax.broadcasted_iota(jnp.int32, (N_MAPS, width), 0)
    lane = lax.broadcasted_iota(jnp.int32, (N_MAPS, width), 1)
    qbd = jnp.where(lane // HEAD_DIM == map_row, jnp.broadcast_to(q_ref[...], (N_MAPS, width)), 0.0)
    qb = qbd.astype(BF16)

    s = jnp.concatenate(
        [jnp.dot(qb, k_refs[i][...].astype(BF16), preferred_element_type=F32) for i in range(pages)],
        axis=1)
    m_prev = m_scr[...]
    m_next = jnp.maximum(m_prev, jnp.max(s, axis=1, keepdims=True))
    alpha = jnp.exp(m_prev - m_next)
    p = jnp.exp(s - _repeat(m_next, pages, axis=1))
    l_scr[...] = alpha * l_scr[...] + jnp.sum(p, axis=1, keepdims=True)
    pb = p.astype(BF16)
    heads = []
    for h in range(N_DH):
        pv = None
        for i in range(pages):
            v_h = v_refs[i][pl.ds(h, PAGE_SIZE, stride=N_DH), :].astype(BF16)
            part = jnp.dot(pb[:, i * PAGE_SIZE:(i + 1) * PAGE_SIZE], v_h, preferred_element_type=F32)
            pv = part if pv is None else pv + part
        heads.append(pv)
    acc_scr[...] = (_repeat(alpha, width // LANES, axis=1) * acc_scr[...]
                    + jnp.concatenate(heads, axis=1))
    m_scr[...] = m_next

    @pl.when(c == pl.num_programs(1) - 1)
    def _():
        s_new = jnp.sum(qbd * kn_ref[...], axis=1, keepdims=True)
        m_prev = m_scr[...]
        m_fin = jnp.maximum(m_prev, s_new)
        alpha = jnp.exp(m_prev - m_fin)
        p_new = jnp.exp(s_new - m_fin)
        l_fin = alpha * l_scr[...] + p_new
        acc = (_repeat(alpha, width // LANES, axis=1) * acc_scr[...]
               + _repeat(p_new, width // LANES, axis=1) * vn_ref[...])
        res = acc / _repeat(l_fin, width // LANES, axis=1)
        head = lane // V_DIM
        first = jnp.sum(jnp.where(map_row == 2 * head, res, 0.0), axis=0, keepdims=True)
        second = jnp.sum(jnp.where(map_row == 2 * head + 1, res, 0.0), axis=0, keepdims=True)
        lam = _lam_value(lq1_ref, lk1_ref, lq2_ref, lk2_ref, lam_init)
        o = first - lam * second
        g = g_ref[...]
        o_ref[...] = jnp.concatenate(
            [_rms_norm(o[:, h * V_DIM:(h + 1) * V_DIM], g) for h in range(N_DH)],
            axis=1) * (1.0 - lam_init)


def _attn_decode(q, k_new, v_new, cache_k, cache_v, page_table, layer, lam_w, lam_init):
    n, _, width = q.shape
    n_pages = page_table.shape[1]
    pages = DECODE_PAGES
    row = pl.BlockSpec((None, 1, width), lambda b, c, pt: (b, 0, 0))
    lam_specs = [pl.BlockSpec((None, 1, HEAD_DIM), lambda b, c, pt: (layer, 0, 0))] * 4
    lam_specs.append(pl.BlockSpec((None, 1, V_DIM), lambda b, c, pt: (layer, 0, 0)))

    def page_spec(i, shape):
        return pl.BlockSpec((None, None) + shape,
                            lambda b, c, pt: (layer, pt[b * n_pages + c * pages + i], 0, 0))

    k_pages = [page_spec(i, (width, PAGE_SIZE)) for i in range(pages)]
    v_pages = [page_spec(i, (PAGE_SIZE * N_DH, V_DIM)) for i in range(pages)]

    grid_spec = pltpu.PrefetchScalarGridSpec(
        num_scalar_prefetch=1,
        grid=(n, n_pages // pages),
        in_specs=[row, row, row] + lam_specs + k_pages + v_pages,
        out_specs=row,
        scratch_shapes=[pltpu.VMEM((N_MAPS, LANES), F32), pltpu.VMEM((N_MAPS, LANES), F32),
                        pltpu.VMEM((N_MAPS, width), F32)],
    )
    return pl.pallas_call(
        functools.partial(_attn_decode_kernel, lam_init=lam_init, pages=pages),
        grid_spec=grid_spec,
        out_shape=jax.ShapeDtypeStruct((n, 1, width), F32),
        compiler_params=_params(("parallel", "arbitrary")),
    )(page_table.reshape(-1), q, k_new, v_new, *lam_w, *([cache_k] * pages), *([cache_v] * pages))


def _merge_kernel(x_ref, rnn_ref, att_ref, gr_ref, ga_ref, g1_ref, wpr_ref, wpa_ref, wout_ref, o_ref):
    pr = jnp.dot(rnn_ref[...].astype(BF16), wpr_ref[...], preferred_element_type=F32)
    pa = jnp.dot(att_ref[...].astype(BF16), wpa_ref[...], preferred_element_type=F32)
    merged = (jax.nn.sigmoid(gr_ref[...].astype(F32)) * pr
              + jax.nn.sigmoid(ga_ref[...].astype(F32)) * pa)
    out = jnp.dot(merged.astype(BF16), wout_ref[...], preferred_element_type=F32)
    o_ref[...] = x_ref[...] + g1_ref[...] * out


def _merge(x, rnn_out, att, gr, ga, mod, layer, w_pr, w_pa, w_out, *, tm):
    groups, rows, d = x.shape
    tile = lambda a: pl.BlockSpec((None, tm, a.shape[2]), lambda g, i: (g, i, 0))
    weight = lambda w: pl.BlockSpec((None,) + w.shape[1:], lambda g, i: (layer, 0, 0))
    return pl.pallas_call(
        _merge_kernel,
        grid=(groups, rows // tm),
        in_specs=[tile(x), tile(rnn_out), tile(att), tile(gr), tile(ga),
                  _mod_spec(layer, 2, mod.shape[2], d), weight(w_pr), weight(w_pa), weight(w_out)],
        out_specs=tile(x),
        out_shape=jax.ShapeDtypeStruct(x.shape, F32),
        compiler_params=_params(("parallel", "parallel")),
    )(x, rnn_out, att, gr, ga, mod, w_pr, w_pa, w_out)


def _mlp_kernel(x_ref, sh_ref, sc_ref, g2_ref, ng_ref, wup_ref, wdn_ref, *rest, final):
    x = x_ref[...]
    hb = (_rms_norm(x, ng_ref[...]) * (1.0 + sc_ref[...]) + sh_ref[...]).astype(BF16)
    d_ff = wup_ref.shape[1]
    acc = jnp.zeros(x.shape, F32)
    for lo in range(0, d_ff, FF_CHUNK):
        u = jnp.maximum(jnp.dot(hb, wup_ref[:, lo:lo + FF_CHUNK], preferred_element_type=F32), 0.0)
        acc = acc + jnp.dot((u * u).astype(BF16), wdn_ref[lo:lo + FF_CHUNK, :],
                            preferred_element_type=F32)
    y = x + g2_ref[...] * acc
    if final:
        fg_ref, o_ref = rest
        o_ref[...] = _rms_norm(y, fg_ref[...])
    else:
        (o_ref,) = rest
        o_ref[...] = y


def _mlp(x, mod, layer, norm_g, w_up, w_down, final_g, *, tm, final):
    groups, rows, d = x.shape
    tile = pl.BlockSpec((None, tm, d), lambda g, i: (g, i, 0))
    weight = lambda w: pl.BlockSpec((None,) + w.shape[1:], lambda g, i: (layer, 0, 0))
    mod_rows = mod.shape[2]
    in_specs = [tile, _mod_spec(layer, 3, mod_rows, d), _mod_spec(layer, 4, mod_rows, d),
                _mod_spec(layer, 5, mod_rows, d), pl.BlockSpec((None, 1, d), lambda g, i: (layer, 0, 0)),
                weight(w_up), weight(w_down)]
    args = [x, mod, mod, mod, norm_g, w_up, w_down]
    if final:
        in_specs.append(pl.BlockSpec((1, d), lambda g, i: (0, 0)))
        args.append(final_g)
    return pl.pallas_call(
        functools.partial(_mlp_kernel, final=final),
        grid=(groups, rows // tm),
        in_specs=in_specs,
        out_specs=tile,
        out_shape=jax.ShapeDtypeStruct(x.shape, F32),
        compiler_params=_params(("parallel", "parallel")),
    )(*args)


def kernel(x_prompt, x_sample, c_prompt, c_sample, cache_k, cache_v, state_rnn, state_conv, page_table, w_ada, b_ada, norm1_g, norm2_g, w_in, conv_w, conv_b, w_a, b_a, w_x, b_x, lru_lambda, lq1, lk1, lq2, lk2, subln_g, w_pr, w_pa, w_out, w_up, w_down, final_g):
    depth, d = norm1_g.shape
    batch, seq, _ = x_prompt.shape
    n_dec = x_sample.shape[0]
    d_att = N_DH * 2 * HEAD_DIM
    assert x_sample.shape[1] == 1, "one new token per running sequence"
    assert seq % ROW_TILE == 0 and seq % RNN_CHUNK == 0 and seq % ATTN_TILE == 0
    assert page_table.shape[1] % DECODE_PAGES == 0 and cache_k.shape[2] == PAGE_SIZE

    w_in_b, w_pr_b, w_pa_b, w_out_b, w_up_b, w_down_b, w_a_b, w_x_b = (
        w.astype(BF16) for w in (w_in, w_pr, w_pa, w_out, w_up, w_down, w_a, w_x))
    row3 = lambda a: a.reshape(depth, 1, a.shape[-1])
    rnn_w = (conv_w, row3(conv_b), w_a_b, row3(b_a), w_x_b, row3(b_x), row3(lru_lambda))
    lam_w = (row3(lq1), row3(lk1), row3(lq2), row3(lk2), row3(subln_g))
    norm1, norm2 = row3(norm1_g), row3(norm2_g)
    final_g2 = final_g.reshape(1, d)

    n_cond = n_dec + batch
    pad = -n_cond % (2 * SUBLANES)
    c_all = jnp.concatenate([c_sample, c_prompt, jnp.zeros((pad, d), F32)], axis=0)
    mods = _ada_mod(c_all, w_ada, b_ada)
    mod_s = mods[:, :n_dec].reshape(depth, 1, n_dec, 6 * d)
    mod_p = mods[:, n_dec:n_cond].reshape(depth, batch, 1, 6 * d)

    ck = jnp.transpose(cache_k, (0, 1, 3, 4, 5, 2)).reshape(depth, cache_k.shape[1], d_att, PAGE_SIZE)
    cv = cache_v.reshape(depth, cache_v.shape[1], PAGE_SIZE * N_DH, V_DIM)
    hist_t = jnp.swapaxes(state_conv, 1, 2)

    xp = x_prompt
    xs = x_sample.reshape(1, n_dec, d)
    h_p, cv_p, k_s, v_s, h_s, cv_s = ([] for _ in range(6))
    kv_p = None
    for l in range(depth):
        lam_init = 0.8 - 0.6 * math.exp(-0.3 * l)
        last = l == depth - 1

        xr, yr, qt, k_all, v_all, gr, ga, kb, vtb = _in_proj(
            xp, mod_p, l, norm1, w_in_b, tm=ROW_TILE, prompt=True, stacked=kv_p)
        kv_p = (k_all, v_all)
        rnn_out, h_last, conv_last = _rnn_prompt(xr, yr, l, rnn_w)
        att = _attn_prompt(qt, kb, vtb, l, lam_w, lam_init)
        xp = _merge(xp, rnn_out, att, gr, ga, mod_p, l, w_pr_b, w_pa_b, w_out_b, tm=ROW_TILE)
        xp = _mlp(xp, mod_p, l, norm2, w_up_b, w_down_b, final_g2, tm=ROW_TILE, final=last)
        h_p.append(h_last[:, 0]); cv_p.append(conv_last)

        xr, yr, q, k, v, gr, ga = _in_proj(xs, mod_s, l, norm1, w_in_b, tm=n_dec, prompt=False)
        rnn_out, h_new, conv_new = _rnn_sample(xr[0], yr[0], hist_t, state_rnn, l, rnn_w)
        as_rows = lambda a: a.reshape(n_dec, 1, a.shape[-1])
        att = _attn_decode(as_rows(q), as_rows(k), as_rows(v), ck, cv, page_table, l, lam_w, lam_init)
        xs = _merge(xs, rnn_out[None], att.reshape(1, n_dec, -1), gr, ga, mod_s, l,
                    w_pr_b, w_pa_b, w_out_b, tm=n_dec)
        xs = _mlp(xs, mod_s, l, norm2, w_up_b, w_down_b, final_g2, tm=n_dec, final=last)
        k_s.append(k[0]); v_s.append(v[0]); h_s.append(h_new); cv_s.append(jnp.swapaxes(conv_new, 0, 1))

    kshape = lambda n, t: (depth, n, t, N_DH, 2, HEAD_DIM)
    vshape = lambda n, t: (depth, n, t, N_DH, V_DIM)
    k_prompt = jnp.transpose(kv_p[0].reshape(depth, batch, N_DH, 2, HEAD_DIM, seq), (0, 1, 5, 2, 3, 4))
    return (xp, xs.reshape(n_dec, 1, d),
            k_prompt, kv_p[1].reshape(vshape(batch, seq)),
            jnp.stack(h_p), jnp.stack(cv_p),
            jnp.stack(k_s).reshape(kshape(n_dec, 1)), jnp.stack(v_s).reshape(vshape(n_dec, 1)),
            jnp.stack(h_s), jnp.stack(cv_s))
```

```python
import functools
import math

import jax
import jax.numpy as jnp
from jax import lax
from jax.experimental import pallas as pl
from jax.experimental.pallas import tpu as pltpu

F32 = jnp.float32
BF16 = jnp.bfloat16

EPS = 1e-6
LRU_C = 8.0
N_RNN_BLOCKS = 8
CONV_WIDTH = 4
N_DH = 4
HEAD_DIM = 64
V_DIM = 2 * HEAD_DIM
N_MAPS = 2 * N_DH
PAGE_SIZE = 128
MASK_VALUE = -1e30
LOG2_E = math.log2(math.e)

V7X_VMEM_BYTES = 64 * 1024 * 1024
VMEM_LIMIT_BYTES = V7X_VMEM_BYTES - 8 * 1024 * 1024
SUBLANES = 8
LANES = 128

ROW_TILE = 512
RNN_CHUNK = 256
ATTN_TILE = 256
ATTN_UNROLL = 4
ATTN_HEADS = 4
DECODE_PAGES = 16
ADA_TILE = 1536
FF_CHUNK = 1024


def _params(semantics):
    return pltpu.CompilerParams(dimension_semantics=semantics, vmem_limit_bytes=VMEM_LIMIT_BYTES)


def _repeat(x, n, axis):
    return jnp.concatenate([x] * n, axis=axis)


def _rms_norm(x, g):
    return x * lax.rsqrt(jnp.mean(x * x, axis=-1, keepdims=True) + EPS) * g


def _lam_value(lq1_ref, lk1_ref, lq2_ref, lk2_ref, lam_init):
    s1 = jnp.sum(lq1_ref[...] * lk1_ref[...], axis=1, keepdims=True)
    s2 = jnp.sum(lq2_ref[...] * lk2_ref[...], axis=1, keepdims=True)
    return jnp.exp(s1) - jnp.exp(s2) + lam_init


def _ada_kernel(c_ref, w_ref, b_ref, o_ref):
    c = c_ref[...]
    a = (c * jax.nn.sigmoid(c)).astype(BF16)
    o_ref[...] = jnp.dot(a, w_ref[...].astype(BF16), preferred_element_type=F32) + b_ref[...]


def _ada_mod(c_all, w_ada, b_ada):
    depth, d, n = w_ada.shape
    rows = c_all.shape[0]
    return pl.pallas_call(
        _ada_kernel,
        grid=(depth, n // ADA_TILE),
        in_specs=[
            pl.BlockSpec((rows, d), lambda l, j: (0, 0)),
            pl.BlockSpec((None, d, ADA_TILE), lambda l, j: (l, 0, j)),
            pl.BlockSpec((None, 1, ADA_TILE), lambda l, j: (l, 0, j)),
        ],
        out_specs=pl.BlockSpec((None, rows, ADA_TILE), lambda l, j: (l, 0, j)),
        out_shape=jax.ShapeDtypeStruct((depth, rows, n), F32),
        compiler_params=_params(("arbitrary", "arbitrary")),
    )(c_all, w_ada, b_ada.reshape(depth, 1, n))


def _mod_spec(layer, which, rows, d):
    return pl.BlockSpec((None, None, rows, d), lambda g, i: (layer, g, 0, which))


def _stream_columns(d):
    widths = [d, d, N_DH * 2 * HEAD_DIM, N_DH * 2 * HEAD_DIM, N_DH * V_DIM, d, d]
    los = [sum(widths[:i]) for i in range(len(widths))]
    return [(lo, lo + w) for lo, w in zip(los, widths)]


def _in_proj_kernel(x_ref, sh_ref, sc_ref, g_ref, w_ref, *refs, prompt):
    hn = _rms_norm(x_ref[...], g_ref[...]) * (1.0 + sc_ref[...]) + sh_ref[...]
    hb = hn.astype(BF16)
    cols = _stream_columns(x_ref.shape[1])
    stream = lambda idx: jnp.dot(hb, w_ref[:, cols[idx][0]:cols[idx][1]], preferred_element_type=F32)
    q_scale = HEAD_DIM ** -0.5 * (LOG2_E if prompt else 1.0)
    q = stream(2) * q_scale
    k = stream(3)
    v = stream(4)
    if prompt:
        xr_ref, yr_ref, q_ref, k_ref, v_ref, gr_ref, ga_ref, kb_ref, vb_ref = refs[-9:]
        q_ref[...] = q.T.astype(BF16)
        k_ref[...] = k.T
        kb_ref[...] = k.astype(BF16)
        for h in range(N_DH):
            v_ref[pl.ds(h, v.shape[0], stride=N_DH), :] = v[:, h * V_DIM:(h + 1) * V_DIM]
        vb_ref[...] = v.T.astype(BF16)
    else:
        xr_ref, yr_ref, q_ref, k_ref, v_ref, gr_ref, ga_ref = refs
        q_ref[...] = q
        k_ref[...] = k
        v_ref[...] = v
    xr_ref[...] = stream(0)
    yr_ref[...] = stream(1).astype(yr_ref.dtype)
    gr_ref[...] = stream(5).astype(gr_ref.dtype)
    ga_ref[...] = stream(6).astype(ga_ref.dtype)


def _in_proj(x, mod, layer, norm_g, w_in, *, tm, prompt, stacked=None):
    groups, rows, d = x.shape
    mod_rows = mod.shape[2]
    widths = [hi - lo for lo, hi in _stream_columns(d)]
    inter = BF16 if prompt else F32
    dtypes = [F32, inter, inter, F32, F32, inter, inter]
    tile = lambda w: pl.BlockSpec((None, tm, w), lambda g, i: (g, i, 0))
    tile_t = lambda w: pl.BlockSpec((None, w, tm), lambda g, i: (g, 0, i))
    out_shapes = [jax.ShapeDtypeStruct((groups, rows, w), t) for w, t in zip(widths, dtypes)]
    out_specs = [tile(w) for w in widths]
    in_specs = [
        tile(d),
        _mod_spec(layer, 0, mod_rows, d),
        _mod_spec(layer, 1, mod_rows, d),
        pl.BlockSpec((None, 1, d), lambda g, i: (layer, 0, 0)),
        pl.BlockSpec((None, d, w_in.shape[2]), lambda g, i: (layer, 0, 0)),
    ]
    args = [x, mod, mod, norm_g, w_in]
    aliases = {}
    if prompt:
        depth = w_in.shape[0]
        out_shapes[2] = jax.ShapeDtypeStruct((groups, widths[2], rows), BF16)
        out_specs[2] = tile_t(widths[2])
        out_shapes[3] = jax.ShapeDtypeStruct((depth, groups, widths[3], rows), F32)
        out_specs[3] = pl.BlockSpec((None, None, widths[3], tm), lambda g, i: (layer, g, 0, i))
        out_shapes[4] = jax.ShapeDtypeStruct((depth, groups, rows * N_DH, V_DIM), F32)
        out_specs[4] = pl.BlockSpec((None, None, tm * N_DH, V_DIM), lambda g, i: (layer, g, i, 0))
        out_shapes += [jax.ShapeDtypeStruct((groups, rows, widths[3]), BF16),
                       jax.ShapeDtypeStruct((groups, widths[4], rows), BF16)]
        out_specs += [tile(widths[3]), tile_t(widths[4])]
        if stacked is not None:
            aliases = {len(args): 3, len(args) + 1: 4}
            in_specs += [pl.BlockSpec(memory_space=pl.ANY)] * 2
            args += list(stacked)
    return pl.pallas_call(
        functools.partial(_in_proj_kernel, prompt=prompt),
        grid=(groups, rows // tm),
        in_specs=in_specs,
        out_specs=out_specs,
        out_shape=out_shapes,
        input_output_aliases=aliases,
        compiler_params=_params(("parallel", "parallel")),
    )(*args)


def _sigmoid(z):
    return 0.5 * jnp.tanh(0.5 * z) + 0.5


def _softplus(z):
    return jnp.maximum(z, 0.0) + jnp.log1p(jnp.exp(-jnp.abs(z)))


def _block_diag_dot(xb, w_ref):
    blk = xb.shape[1] // N_RNN_BLOCKS
    return jnp.concatenate(
        [jnp.dot(xb[:, n * blk:(n + 1) * blk], w_ref[n], preferred_element_type=F32)
         for n in range(N_RNN_BLOCKS)], axis=1)


def _lru_terms(xc, wa_ref, ba_ref, wx_ref, bx_ref, lam_ref):
    xb = xc.astype(BF16)
    r = _sigmoid(_block_diag_dot(xb, wa_ref) + ba_ref[...])
    i = _sigmoid(_block_diag_dot(xb, wx_ref) + bx_ref[...])
    log_a = -LRU_C * r * _softplus(-lam_ref[...])
    a = jnp.exp(log_a)
    u = -jnp.tanh(log_a) * (a * a + 1.0)
    b = jnp.where(u > 0.0, u * lax.rsqrt(u), 0.0) * (i * xc)
    return a, b


def _gelu_tanh(y):
    return 0.5 * y * (1.0 + jnp.tanh(math.sqrt(2.0 / math.pi) * (y + 0.044715 * (y * y * y))))


def _rnn_prompt_kernel(xr_ref, yr_ref, cw_ref, cb_ref, wa_ref, ba_ref, wx_ref, bx_ref, lam_ref,
                       out_ref, hlast_ref, conv_ref, xbuf, h_scr):
    t = pl.program_id(1)
    steps = xr_ref.shape[0]
    hist = SUBLANES

    @pl.when(t == 0)
    def _():
        xbuf[0:hist, :] = jnp.zeros((hist, xbuf.shape[1]), F32)
        h_scr[...] = jnp.zeros_like(h_scr)

    x = xr_ref[...]
    xbuf[hist:hist + steps, :] = x
    xc = cb_ref[...] + cw_ref[CONV_WIDTH - 1:CONV_WIDTH, :] * x
    for j in range(CONV_WIDTH - 1):
        back = CONV_WIDTH - 1 - j
        xc = xc + cw_ref[j:j + 1, :] * xbuf[hist - back:hist - back + steps, :]
    xbuf[0:hist, :] = x[steps - hist:steps, :]

    a, b = _lru_terms(xc, wa_ref, ba_ref, wx_ref, bx_ref, lam_ref)
    groups = steps // SUBLANES
    a = a.reshape(groups, SUBLANES, a.shape[1])
    b = b.reshape(groups, SUBLANES, b.shape[1])
    sub = lax.broadcasted_iota(jnp.int32, (1, SUBLANES, 1), 1)
    shift = 1
    while shift < SUBLANES:
        a_prev = pltpu.roll(a, shift, axis=1)
        b_prev = pltpu.roll(b, shift, axis=1)
        valid = sub >= shift
        b = jnp.where(valid, a * b_prev + b, b)
        a = jnp.where(valid, a * a_prev, a)
        shift *= 2
    h_prev = h_scr[...]
    hs = []
    for g in range(groups):
        h_g = a[g] * h_prev + b[g]
        hs.append(h_g)
        h_prev = h_g[SUBLANES - 1:SUBLANES, :]
    h = jnp.concatenate(hs, axis=0)
    h_scr[...] = h_prev
    out_ref[...] = (h * _gelu_tanh(yr_ref[...].astype(F32))).astype(out_ref.dtype)

    @pl.when(t == pl.num_programs(1) - 1)
    def _():
        hlast_ref[...] = h[steps - 1:steps, :]
        conv_ref[...] = x[steps - (CONV_WIDTH - 1):steps, :]


def _rnn_weight_specs(layer, c):
    idx2 = lambda *_: (layer, 0, 0)
    idx3 = lambda *_: (layer, 0, 0, 0)
    blk = c // N_RNN_BLOCKS
    return [
        pl.BlockSpec((None, CONV_WIDTH, c), idx2),
        pl.BlockSpec((None, 1, c), idx2),
        pl.BlockSpec((None, N_RNN_BLOCKS, blk, blk), idx3),
        pl.BlockSpec((None, 1, c), idx2),
        pl.BlockSpec((None, N_RNN_BLOCKS, blk, blk), idx3),
        pl.BlockSpec((None, 1, c), idx2),
        pl.BlockSpec((None, 1, c), idx2),
    ]


def _rnn_prompt(xr, yr, layer, rnn_w):
    batch, seq, c = xr.shape
    tile = pl.BlockSpec((None, RNN_CHUNK, c), lambda b, t: (b, t, 0))
    return pl.pallas_call(
        _rnn_prompt_kernel,
        grid=(batch, seq // RNN_CHUNK),
        in_specs=[tile, tile] + _rnn_weight_specs(layer, c),
        out_specs=[
            tile,
            pl.BlockSpec((None, 1, c), lambda b, t: (b, 0, 0)),
            pl.BlockSpec((None, CONV_WIDTH - 1, c), lambda b, t: (b, 0, 0)),
        ],
        out_shape=[
            jax.ShapeDtypeStruct((batch, seq, c), BF16),
            jax.ShapeDtypeStruct((batch, 1, c), F32),
            jax.ShapeDtypeStruct((batch, CONV_WIDTH - 1, c), F32),
        ],
        scratch_shapes=[pltpu.VMEM((SUBLANES + RNN_CHUNK, c), F32), pltpu.VMEM((1, c), F32)],
        compiler_params=_params(("parallel", "arbitrary")),
    )(xr, yr, *rnn_w)


def _rnn_sample_kernel(xr_ref, yr_ref, hist_ref, h0_ref, cw_ref, cb_ref, wa_ref, ba_ref, wx_ref,
                       bx_ref, lam_ref, out_ref, hnew_ref, conv_ref):
    x = xr_ref[...]
    xc = cb_ref[...] + cw_ref[CONV_WIDTH - 1:CONV_WIDTH, :] * x
    for j in range(CONV_WIDTH - 1):
        xc = xc + cw_ref[j:j + 1, :] * hist_ref[j]
    a, b = _lru_terms(xc, wa_ref, ba_ref, wx_ref, bx_ref, lam_ref)
    h = a * h0_ref[...] + b
    hnew_ref[...] = h
    out_ref[...] = h * _gelu_tanh(yr_ref[...])
    for j in range(CONV_WIDTH - 2):
        conv_ref[j] = hist_ref[j + 1]
    conv_ref[CONV_WIDTH - 2] = x


def _rnn_sample(xr, yr, hist_t, h0, layer, rnn_w):
    n, c = xr.shape
    full = pl.BlockSpec((n, c), lambda i: (0, 0))
    hist_spec = pl.BlockSpec((None, CONV_WIDTH - 1, n, c), lambda i: (layer, 0, 0, 0))
    return pl.pallas_call(
        _rnn_sample_kernel,
        grid=(1,),
        in_specs=[full, full, hist_spec, pl.BlockSpec((None, n, c), lambda i: (layer, 0, 0))]
        + _rnn_weight_specs(layer, c),
        out_specs=[full, full, pl.BlockSpec((CONV_WIDTH - 1, n, c), lambda i: (0, 0, 0))],
        out_shape=[
            jax.ShapeDtypeStruct((n, c), F32),
            jax.ShapeDtypeStruct((n, c), F32),
            jax.ShapeDtypeStruct((CONV_WIDTH - 1, n, c), F32),
        ],
        compiler_params=_params(("arbitrary",)),
    )(xr, yr, hist_t, h0, *rnn_w)


def _attn_prompt_kernel(q_ref, k_ref, v_ref, lq1_ref, lk1_ref, lq2_ref, lk2_ref, g_ref, o_ref,
                        m_scr, l_scr, acc_scr, s_scr, *, lam_init):
    i = pl.program_id(2)
    tq = q_ref.shape[1]
    tk = tq
    feats = 2 * HEAD_DIM
    heads = q_ref.shape[0] // feats
    feat = lax.broadcasted_iota(jnp.int32, (feats, tq), 0)
    qq = []
    for h in range(heads):
        qt = q_ref[h * feats:(h + 1) * feats, :]
        zero = jnp.zeros_like(qt)
        qq.append(jnp.concatenate([jnp.where(feat < HEAD_DIM, qt, zero),
                                   jnp.where(feat >= HEAD_DIM, qt, zero)], axis=1))
    m_scr[...] = jnp.full(m_scr.shape, MASK_VALUE, F32)
    l_scr[...] = jnp.zeros_like(l_scr)
    acc_scr[...] = jnp.zeros_like(acc_scr)

    def scores(h, j, slot):
        start = pl.multiple_of(j * tk, tk)
        kj = k_ref[pl.ds(start, tk), h * feats:(h + 1) * feats]
        s_scr[h, slot] = jnp.dot(kj, qq[h], preferred_element_type=F32)

    def update(h, j, slot, masked):
        start = pl.multiple_of(j * tk, tk)
        vj = v_ref[h * V_DIM:(h + 1) * V_DIM, pl.ds(start, tk)]
        s = s_scr[h, slot]
        if masked:
            kpos = lax.broadcasted_iota(jnp.int32, s.shape, 0)
            c = lax.broadcasted_iota(jnp.int32, s.shape, 1)
            qpos = jnp.where(c >= tq, c - tq, c)
            s = jnp.where(kpos <= qpos, s, MASK_VALUE)
        m_prev = m_scr[h]
        m_next = jnp.maximum(m_prev, jnp.max(s, axis=0, keepdims=True))
        alpha = jnp.exp2(m_prev - m_next)
        p = jnp.exp2(s - m_next)
        l_scr[h] = alpha * l_scr[h] + jnp.sum(p, axis=0, keepdims=True)
        acc_scr[h] = alpha * acc_scr[h] + jnp.dot(vj, p.astype(BF16), preferred_element_type=F32)
        m_scr[h] = m_next

    def advance(j, slot, masked):
        for h in range(heads):
            if not masked:
                scores(h, j + 1, 1 - slot)
            other = (h + 1) % heads
            update(other, j, slot, masked)

    def group(g, carry):
        for u in range(ATTN_UNROLL):
            advance(g * ATTN_UNROLL + u, u % 2, False)
        return carry

    for h in range(heads):
        scores(h, 0, 0)
    n_groups = i // ATTN_UNROLL
    lax.fori_loop(0, n_groups, group, 0)
    done = n_groups * ATTN_UNROLL
    width = ATTN_UNROLL // 2
    while width >= 2:
        @pl.when((i & width) != 0)
        def _(done=done, width=width):
            for u in range(width):
                advance(done + u, u % 2, False)
        done = done + (i & width)
        width //= 2

    @pl.when((i & 1) != 0)
    def _():
        advance(done, 0, False)
        advance(done + 1, 1, True)

    @pl.when((i & 1) == 0)
    def _():
        advance(done, 0, True)

    lam = _lam_value(lq1_ref, lk1_ref, lq2_ref, lk2_ref, lam_init)
    for h in range(heads):
        o = acc_scr[h] / l_scr[h]
        o = (o[:, :tq] - lam * o[:, tq:]).T
        o_ref[:, h * V_DIM:(h + 1) * V_DIM] = (
            _rms_norm(o, g_ref[...]) * (1.0 - lam_init)).astype(o_ref.dtype)


def _lam_specs(layer):
    idx = lambda *_: (layer, 0, 0)
    return [pl.BlockSpec((None, 1, HEAD_DIM), idx)] * 4 + [pl.BlockSpec((None, 1, V_DIM), idx)]


def _attn_prompt(q, k, v, layer, lam_w, lam_init):
    batch, seq, _ = k.shape
    tq = ATTN_TILE
    hs = ATTN_HEADS
    q_spec = pl.BlockSpec((None, hs * 2 * HEAD_DIM, tq), lambda b, h, i: (b, h, i))
    k_spec = pl.BlockSpec((None, seq, hs * 2 * HEAD_DIM), lambda b, h, i: (b, 0, h))
    v_spec = pl.BlockSpec((None, hs * V_DIM, seq), lambda b, h, i: (b, h, 0))
    return pl.pallas_call(
        functools.partial(_attn_prompt_kernel, lam_init=lam_init),
        grid=(batch, N_DH // hs, seq // tq),
        in_specs=[q_spec, k_spec, v_spec] + _lam_specs(layer),
        out_specs=pl.BlockSpec((None, tq, hs * V_DIM), lambda b, h, i: (b, i, h)),
        out_shape=jax.ShapeDtypeStruct((batch, seq, N_DH * V_DIM), BF16),
        scratch_shapes=[pltpu.VMEM((hs, 1, 2 * tq), F32), pltpu.VMEM((hs, 1, 2 * tq), F32),
                        pltpu.VMEM((hs, V_DIM, 2 * tq), F32), pltpu.VMEM((hs, 2, tq, 2 * tq), F32)],
        compiler_params=_params(("parallel", "parallel", "arbitrary")),
    )(q, k, v, *lam_w)


def _decode_reset(m_scr, l_scr, acc_scr):
    m_scr[...] = jnp.full(m_scr.shape, MASK_VALUE, F32)
    l_scr[...] = jnp.zeros_like(l_scr)
    acc_scr[...] = jnp.zeros_like(acc_scr)


def _decode_query(q_ref):
    width = q_ref.shape[1]
    map_row = lax.broadcasted_iota(jnp.int32, (N_MAPS, width), 0)
    lane = lax.broadcasted_iota(jnp.int32, (N_MAPS, width), 1)
    qbd = jnp.where(lane // HEAD_DIM == map_row, jnp.broadcast_to(q_ref[...], (N_MAPS, width)), 0.0)
    return qbd, map_row, lane


def _decode_pages(qbd, k_refs, v_refs, m_scr, l_scr, acc_scr):
    pages = len(k_refs)
    width = qbd.shape[1]
    qb = qbd.astype(BF16)
    s = jnp.concatenate(
        [jnp.dot(qb, k_refs[i][...].astype(BF16), preferred_element_type=F32) for i in range(pages)],
        axis=1)
    m_prev = m_scr[...]
    m_next = jnp.maximum(m_prev, jnp.max(s, axis=1, keepdims=True))
    alpha = jnp.exp(m_prev - m_next)
    p = jnp.exp(s - _repeat(m_next, pages, axis=1))
    l_scr[...] = alpha * l_scr[...] + jnp.sum(p, axis=1, keepdims=True)
    pb = p.astype(BF16)
    heads = []
    for h in range(N_DH):
        pv = None
        for i in range(pages):
            v_h = v_refs[i][pl.ds(h, PAGE_SIZE, stride=N_DH), :].astype(BF16)
            part = jnp.dot(pb[:, i * PAGE_SIZE:(i + 1) * PAGE_SIZE], v_h, preferred_element_type=F32)
            pv = part if pv is None else pv + part
        heads.append(pv)
    acc_scr[...] = (_repeat(alpha, width // LANES, axis=1) * acc_scr[...]
                    + jnp.concatenate(heads, axis=1))
    m_scr[...] = m_next


def _decode_finish(qbd, map_row, lane, kn_ref, vn_ref, lam_refs, g_ref, m_scr, l_scr, acc_scr, lam_init):
    width = qbd.shape[1]
    s_new = jnp.sum(qbd * kn_ref[...], axis=1, keepdims=True)
    m_prev = m_scr[...]
    m_fin = jnp.maximum(m_prev, s_new)
    alpha = jnp.exp(m_prev - m_fin)
    p_new = jnp.exp(s_new - m_fin)
    l_fin = alpha * l_scr[...] + p_new
    acc = (_repeat(alpha, width // LANES, axis=1) * acc_scr[...]
           + _repeat(p_new, width // LANES, axis=1) * vn_ref[...])
    res = acc / _repeat(l_fin, width // LANES, axis=1)
    head = lane // V_DIM
    first = jnp.sum(jnp.where(map_row == 2 * head, res, 0.0), axis=0, keepdims=True)
    second = jnp.sum(jnp.where(map_row == 2 * head + 1, res, 0.0), axis=0, keepdims=True)
    lam = _lam_value(*lam_refs, lam_init)
    o = first - lam * second
    g = g_ref[...]
    return jnp.concatenate([_rms_norm(o[:, h * V_DIM:(h + 1) * V_DIM], g) for h in range(N_DH)],
                           axis=1) * (1.0 - lam_init)


def _merge_kernel(x_ref, rnn_ref, att_ref, gr_ref, ga_ref, g1_ref, wpr_ref, wpa_ref, wout_ref, o_ref):
    pr = jnp.dot(rnn_ref[...].astype(BF16), wpr_ref[...], preferred_element_type=F32)
    pa = jnp.dot(att_ref[...].astype(BF16), wpa_ref[...], preferred_element_type=F32)
    merged = (jax.nn.sigmoid(gr_ref[...].astype(F32)) * pr
              + jax.nn.sigmoid(ga_ref[...].astype(F32)) * pa)
    out = jnp.dot(merged.astype(BF16), wout_ref[...], preferred_element_type=F32)
    o_ref[...] = x_ref[...] + g1_ref[...] * out


def _merge(x, rnn_out, att, gr, ga, mod, layer, w_pr, w_pa, w_out, *, tm):
    groups, rows, d = x.shape
    tile = lambda a: pl.BlockSpec((None, tm, a.shape[2]), lambda g, i: (g, i, 0))
    weight = lambda w: pl.BlockSpec((None,) + w.shape[1:], lambda g, i: (layer, 0, 0))
    return pl.pallas_call(
        _merge_kernel,
        grid=(groups, rows // tm),
        in_specs=[tile(x), tile(rnn_out), tile(att), tile(gr), tile(ga),
                  _mod_spec(layer, 2, mod.shape[2], d), weight(w_pr), weight(w_pa), weight(w_out)],
        out_specs=tile(x),
        out_shape=jax.ShapeDtypeStruct(x.shape, F32),
        compiler_params=_params(("parallel", "parallel")),
    )(x, rnn_out, att, gr, ga, mod, w_pr, w_pa, w_out)


def _mlp_hidden(x, sh_ref, sc_ref, ng_ref):
    return (_rms_norm(x, ng_ref[...]) * (1.0 + sc_ref[...]) + sh_ref[...]).astype(BF16)


def _ff_chunk(hb, wup_ref, wdn_ref, lo):
    u = jnp.maximum(jnp.dot(hb, wup_ref[:, lo:lo + FF_CHUNK], preferred_element_type=F32), 0.0)
    return jnp.dot((u * u).astype(BF16), wdn_ref[lo:lo + FF_CHUNK, :], preferred_element_type=F32)


def _mlp_finish(x, acc, g2_ref, fg_ref, o_ref):
    y = x + g2_ref[...] * acc
    o_ref[...] = y if fg_ref is None else _rms_norm(y, fg_ref[...])


def _mlp_kernel(x_ref, sh_ref, sc_ref, g2_ref, ng_ref, wup_ref, wdn_ref, *rest, final):
    fg_ref = rest[0] if final else None
    o_ref = rest[-1]
    x = x_ref[...]
    hb = _mlp_hidden(x, sh_ref, sc_ref, ng_ref)
    acc = jnp.zeros(x.shape, F32)
    for lo in range(0, wup_ref.shape[1], FF_CHUNK):
        acc = acc + _ff_chunk(hb, wup_ref, wdn_ref, lo)
    _mlp_finish(x, acc, g2_ref, fg_ref, o_ref)


def _mlp_decode_kernel(pt_ref, x_ref, sh_ref, sc_ref, g2_ref, ng_ref, wup_ref, wdn_ref, *rest,
                       final, lam_init, layer, n_pages):
    fg_ref = rest[0] if final else None
    (q_ref, kn_ref, vn_ref, lq1_ref, lk1_ref, lq2_ref, lk2_ref, sg_ref, ck_ref, cv_ref,
     o_ref, os_ref, kbuf, vbuf, sem, m_scr, l_scr, acc_scr) = rest[1 if final else 0:]
    steps = pl.num_programs(1)
    t = pl.program_id(0) * steps + pl.program_id(1)
    total = pl.num_programs(0) * steps
    n_chunks = n_pages // DECODE_PAGES
    n_ff = wup_ref.shape[1] // FF_CHUNK

    def copies(seq, chunk, slot):
        out = []
        for p in range(DECODE_PAGES):
            page = pt_ref[seq * n_pages + chunk * DECODE_PAGES + p]
            out.append(pltpu.make_async_copy(ck_ref.at[layer, page], kbuf.at[slot, p], sem.at[slot, 0]))
            out.append(pltpu.make_async_copy(cv_ref.at[layer, page], vbuf.at[slot, p], sem.at[slot, 1]))
        return out

    def start(seq, chunk, slot):
        for copy in copies(seq, chunk, slot):
            copy.start()

    def wait(seq, chunk, slot):
        for copy in copies(seq, chunk, slot):
            copy.wait()

    @pl.when(t == 0)
    def _():
        start(0, 0, 0)
        start(0, 1, 1)

    qbd, map_row, lane = _decode_query(q_ref)
    _decode_reset(m_scr, l_scr, acc_scr)
    x = x_ref[...]
    hb = _mlp_hidden(x, sh_ref, sc_ref, ng_ref)
    acc = jnp.zeros(x.shape, F32)
    for c in range(n_chunks):
        slot = c % 2
        wait(t, c, slot)
        _decode_pages(qbd, [kbuf.at[slot, p] for p in range(DECODE_PAGES)],
                      [vbuf.at[slot, p] for p in range(DECODE_PAGES)], m_scr, l_scr, acc_scr)
        if c + 2 < n_chunks:
            start(t, c + 2, slot)
        else:
            @pl.when(t + 1 < total)
            def _(c=c, slot=slot):
                start(t + 1, c + 2 - n_chunks, slot)
        for f in range(c * n_ff // n_chunks, (c + 1) * n_ff // n_chunks):
            acc = acc + _ff_chunk(hb, wup_ref, wdn_ref, f * FF_CHUNK)
    os_ref[...] = _decode_finish(qbd, map_row, lane, kn_ref, vn_ref,
                                 (lq1_ref, lk1_ref, lq2_ref, lk2_ref), sg_ref,
                                 m_scr, l_scr, acc_scr, lam_init)
    _mlp_finish(x, acc, g2_ref, fg_ref, o_ref)


def _mlp(x, mod, layer, norm_g, w_up, w_down, final_g, *, tm, final):
    groups, rows, d = x.shape
    tile = pl.BlockSpec((None, tm, d), lambda g, i: (g, i, 0))
    weight = lambda w: pl.BlockSpec((None,) + w.shape[1:], lambda g, i: (layer, 0, 0))
    mod_rows = mod.shape[2]
    in_specs = [tile, _mod_spec(layer, 3, mod_rows, d), _mod_spec(layer, 4, mod_rows, d),
                _mod_spec(layer, 5, mod_rows, d), pl.BlockSpec((None, 1, d), lambda g, i: (layer, 0, 0)),
                weight(w_up), weight(w_down)]
    args = [x, mod, mod, mod, norm_g, w_up, w_down]
    if final:
        in_specs.append(pl.BlockSpec((1, d), lambda g, i: (0, 0)))
        args.append(final_g)
    return pl.pallas_call(
        functools.partial(_mlp_kernel, final=final),
        grid=(groups, rows // tm),
        in_specs=in_specs,
        out_specs=tile,
        out_shape=jax.ShapeDtypeStruct(x.shape, F32),
        compiler_params=_params(("parallel", "parallel")),
    )(*args)


def _mlp_decode(x, mod, layer, norm_g, w_up, w_down, final_g, q, k_new, v_new, cache_k, cache_v,
                page_table, lam_w, lam_init, *, tm, final):
    groups, rows, d = x.shape
    n, _, width = q.shape
    n_pages = page_table.shape[1]
    steps = rows // tm
    assert n == groups * steps, "one running sequence per MLP grid step"
    assert n_pages % (2 * DECODE_PAGES) == 0, "page chunks alternate between two slots"
    mod_rows = mod.shape[2]
    tile = pl.BlockSpec((None, tm, d), lambda g, i, pt: (g, i, 0))
    mod_spec = lambda which: pl.BlockSpec((None, None, mod_rows, d), lambda g, i, pt: (layer, g, 0, which))
    per_layer = lambda shape, **kw: pl.BlockSpec((None,) + shape, lambda g, i, pt: (layer,) + (0,) * len(shape), **kw)
    row = pl.BlockSpec((None, 1, width), lambda g, i, pt: (g * steps + i, 0, 0))
    hbm = pl.BlockSpec(memory_space=pl.ANY)
    in_specs = [tile, mod_spec(3), mod_spec(4), mod_spec(5), per_layer((1, d)),
                per_layer(w_up.shape[1:], pipeline_mode=pl.Buffered(1)),
                per_layer(w_down.shape[1:], pipeline_mode=pl.Buffered(1))]
    args = [x, mod, mod, mod, norm_g, w_up, w_down]
    if final:
        in_specs.append(pl.BlockSpec((1, d), lambda g, i, pt: (0, 0)))
        args.append(final_g)
    in_specs += [row, row, row] + [per_layer((1, HEAD_DIM))] * 4 + [per_layer((1, V_DIM)), hbm, hbm]
    args += [q, k_new, v_new, *lam_w, cache_k, cache_v]
    grid_spec = pltpu.PrefetchScalarGridSpec(
        num_scalar_prefetch=1,
        grid=(groups, steps),
        in_specs=in_specs,
        out_specs=[tile, row],
        scratch_shapes=[pltpu.VMEM((2, DECODE_PAGES, width, PAGE_SIZE), F32),
                        pltpu.VMEM((2, DECODE_PAGES, PAGE_SIZE * N_DH, V_DIM), F32),
                        pltpu.SemaphoreType.DMA((2, 2)),
                        pltpu.VMEM((N_MAPS, LANES), F32), pltpu.VMEM((N_MAPS, LANES), F32),
                        pltpu.VMEM((N_MAPS, width), F32)],
    )
    return pl.pallas_call(
        functools.partial(_mlp_decode_kernel, final=final, lam_init=lam_init, layer=layer,
                          n_pages=n_pages),
        grid_spec=grid_spec,
        out_shape=[jax.ShapeDtypeStruct(x.shape, F32), jax.ShapeDtypeStruct((n, 1, width), F32)],
        compiler_params=_params(("arbitrary", "arbitrary")),
    )(page_table.reshape(-1), *args)


def kernel(x_prompt, x_sample, c_prompt, c_sample, cache_k, cache_v, state_rnn, state_conv, page_table, w_ada, b_ada, norm1_g, norm2_g, w_in, conv_w, conv_b, w_a, b_a, w_x, b_x, lru_lambda, lq1, lk1, lq2, lk2, subln_g, w_pr, w_pa, w_out, w_up, w_down, final_g):
    depth, d = norm1_g.shape
    batch, seq, _ = x_prompt.shape
    n_dec = x_sample.shape[0]
    d_att = N_DH * 2 * HEAD_DIM
    assert x_sample.shape[1] == 1, "one new token per running sequence"
    assert seq % ROW_TILE == 0 and seq % RNN_CHUNK == 0 and seq % ATTN_TILE == 0
    assert page_table.shape[1] % DECODE_PAGES == 0 and cache_k.shape[2] == PAGE_SIZE

    w_in_b, w_pr_b, w_pa_b, w_out_b, w_up_b, w_down_b, w_a_b, w_x_b = (
        w.astype(BF16) for w in (w_in, w_pr, w_pa, w_out, w_up, w_down, w_a, w_x))
    row3 = lambda a: a.reshape(depth, 1, a.shape[-1])
    rnn_w = (conv_w, row3(conv_b), w_a_b, row3(b_a), w_x_b, row3(b_x), row3(lru_lambda))
    lam_w = (row3(lq1), row3(lk1), row3(lq2), row3(lk2), row3(subln_g))
    norm1, norm2 = row3(norm1_g), row3(norm2_g)
    final_g2 = final_g.reshape(1, d)

    n_cond = n_dec + batch
    pad = -n_cond % (2 * SUBLANES)
    c_all = jnp.concatenate([c_sample, c_prompt, jnp.zeros((pad, d), F32)], axis=0)
    mods = _ada_mod(c_all, w_ada, b_ada)
    mod_s = mods[:, :n_dec].reshape(depth, 1, n_dec, 6 * d)
    mod_p = mods[:, n_dec:n_cond].reshape(depth, batch, 1, 6 * d)

    ck = jnp.transpose(cache_k, (0, 1, 3, 4, 5, 2)).reshape(depth, cache_k.shape[1], d_att, PAGE_SIZE)
    cv = cache_v.reshape(depth, cache_v.shape[1], PAGE_SIZE * N_DH, V_DIM)
    hist_t = jnp.swapaxes(state_conv, 1, 2)

    xp = x_prompt
    xs = x_sample.reshape(1, n_dec, d)
    h_p, cv_p, k_s, v_s, h_s, cv_s = ([] for _ in range(6))
    kv_p = None
    for l in range(depth):
        lam_init = 0.8 - 0.6 * math.exp(-0.3 * l)
        last = l == depth - 1

        xr, yr, qt, k_all, v_all, gr, ga, kb, vtb = _in_proj(
            xp, mod_p, l, norm1, w_in_b, tm=ROW_TILE, prompt=True, stacked=kv_p)
        kv_p = (k_all, v_all)
        rnn_out, h_last, conv_last = _rnn_prompt(xr, yr, l, rnn_w)
        att = _attn_prompt(qt, kb, vtb, l, lam_w, lam_init)
        xp = _merge(xp, rnn_out, att, gr, ga, mod_p, l, w_pr_b, w_pa_b, w_out_b, tm=ROW_TILE)
        h_p.append(h_last[:, 0]); cv_p.append(conv_last)

        xr, yr, q, k, v, gr, ga = _in_proj(xs, mod_s, l, norm1, w_in_b, tm=n_dec, prompt=False)
        rnn_out, h_new, conv_new = _rnn_sample(xr[0], yr[0], hist_t, state_rnn, l, rnn_w)
        as_rows = lambda a: a.reshape(n_dec, 1, a.shape[-1])
        xp, att = _mlp_decode(xp, mod_p, l, norm2, w_up_b, w_down_b, final_g2, as_rows(q), as_rows(k),
                              as_rows(v), ck, cv, page_table, lam_w, lam_init, tm=ROW_TILE, final=last)
        xs = _merge(xs, rnn_out[None], att.reshape(1, n_dec, -1), gr, ga, mod_s, l,
                    w_pr_b, w_pa_b, w_out_b, tm=n_dec)
        xs = _mlp(xs, mod_s, l, norm2, w_up_b, w_down_b, final_g2, tm=n_dec, final=last)
        k_s.append(k[0]); v_s.append(v[0]); h_s.append(h_new); cv_s.append(jnp.swapaxes(conv_new, 0, 1))

    kshape = lambda n, t: (depth, n, t, N_DH, 2, HEAD_DIM)
    vshape = lambda n, t: (depth, n, t, N_DH, V_DIM)
    k_prompt = jnp.transpose(kv_p[0].reshape(depth, batch, N_DH, 2, HEAD_DIM, seq), (0, 1, 5, 2, 3, 4))
    return (xp, xs.reshape(n_dec, 1, d),
            k_prompt, kv_p[1].reshape(vshape(batch, seq)),
            jnp.stack(h_p), jnp.stack(cv_p),
            jnp.stack(k_s).reshape(kshape(n_dec, 1)), jnp.stack(v_s).reshape(vshape(n_dec, 1)),
            jnp.stack(h_s), jnp.stack(cv_s))
```

```python
import functools
import math

import jax
import jax.numpy as jnp
from jax import lax
from jax.experimental import pallas as pl
from jax.experimental.pallas import tpu as pltpu

F32 = jnp.float32
BF16 = jnp.bfloat16

EPS = 1e-6
LRU_C = 8.0
N_RNN_BLOCKS = 8
CONV_WIDTH = 4
N_DH = 4
HEAD_DIM = 64
V_DIM = 2 * HEAD_DIM
N_MAPS = 2 * N_DH
PAGE_SIZE = 128
MASK_VALUE = -1e30
LOG2_E = math.log2(math.e)

V7X_VMEM_BYTES = 64 * 1024 * 1024
VMEM_LIMIT_BYTES = V7X_VMEM_BYTES - 8 * 1024 * 1024
SUBLANES = 8
LANES = 128

ROW_TILE = 512
RNN_CHUNK = 256
ATTN_TILE = 256
ATTN_UNROLL = 4
ATTN_HEADS = 4
DECODE_PAGES = 16
ADA_TILE = 1536
FF_CHUNK = 1024


def _params(semantics):
    return pltpu.CompilerParams(dimension_semantics=semantics, vmem_limit_bytes=VMEM_LIMIT_BYTES)


def _repeat(x, n, axis):
    return jnp.concatenate([x] * n, axis=axis)


def _rms_norm(x, g):
    return x * lax.rsqrt(jnp.mean(x * x, axis=-1, keepdims=True) + EPS) * g


def _lam_value(lq1_ref, lk1_ref, lq2_ref, lk2_ref, lam_init):
    s1 = jnp.sum(lq1_ref[...] * lk1_ref[...], axis=1, keepdims=True)
    s2 = jnp.sum(lq2_ref[...] * lk2_ref[...], axis=1, keepdims=True)
    return jnp.exp(s1) - jnp.exp(s2) + lam_init


def _ada_kernel(c_ref, w_ref, b_ref, o_ref):
    c = c_ref[...]
    a = (c * jax.nn.sigmoid(c)).astype(BF16)
    o_ref[...] = jnp.dot(a, w_ref[...].astype(BF16), preferred_element_type=F32) + b_ref[...]


def _ada_mod(c_all, w_ada, b_ada):
    depth, d, n = w_ada.shape
    rows = c_all.shape[0]
    return pl.pallas_call(
        _ada_kernel,
        grid=(depth, n // ADA_TILE),
        in_specs=[
            pl.BlockSpec((rows, d), lambda l, j: (0, 0)),
            pl.BlockSpec((None, d, ADA_TILE), lambda l, j: (l, 0, j)),
            pl.BlockSpec((None, 1, ADA_TILE), lambda l, j: (l, 0, j)),
        ],
        out_specs=pl.BlockSpec((None, rows, ADA_TILE), lambda l, j: (l, 0, j)),
        out_shape=jax.ShapeDtypeStruct((depth, rows, n), F32),
        compiler_params=_params(("arbitrary", "arbitrary")),
    )(c_all, w_ada, b_ada.reshape(depth, 1, n))


def _mod_spec(layer, which, rows, d):
    return pl.BlockSpec((None, None, rows, d), lambda g, i: (layer, g, 0, which))


def _stream_columns(d):
    widths = [d, d, N_DH * 2 * HEAD_DIM, N_DH * 2 * HEAD_DIM, N_DH * V_DIM, d, d]
    los = [sum(widths[:i]) for i in range(len(widths))]
    return [(lo, lo + w) for lo, w in zip(los, widths)]


def _in_proj_kernel(x_ref, sh_ref, sc_ref, g_ref, w_ref, *refs, prompt):
    hn = _rms_norm(x_ref[...], g_ref[...]) * (1.0 + sc_ref[...]) + sh_ref[...]
    hb = hn.astype(BF16)
    cols = _stream_columns(x_ref.shape[1])
    stream = lambda idx: jnp.dot(hb, w_ref[:, cols[idx][0]:cols[idx][1]], preferred_element_type=F32)
    q_scale = HEAD_DIM ** -0.5 * (LOG2_E if prompt else 1.0)
    q = stream(2) * q_scale
    k = stream(3)
    v = stream(4)
    if prompt:
        xr_ref, yr_ref, q_ref, k_ref, v_ref, gr_ref, ga_ref, kb_ref, vb_ref = refs[-9:]
        q_ref[...] = q.T.astype(BF16)
        k_ref[...] = k.T
        kb_ref[...] = k.astype(BF16)
        for h in range(N_DH):
            v_ref[pl.ds(h, v.shape[0], stride=N_DH), :] = v[:, h * V_DIM:(h + 1) * V_DIM]
        vb_ref[...] = v.T.astype(BF16)
    else:
        xr_ref, yr_ref, q_ref, k_ref, v_ref, gr_ref, ga_ref = refs
        q_ref[...] = q
        k_ref[...] = k
        v_ref[...] = v
    xr_ref[...] = stream(0)
    yr_ref[...] = stream(1).astype(yr_ref.dtype)
    gr_ref[...] = stream(5).astype(gr_ref.dtype)
    ga_ref[...] = stream(6).astype(ga_ref.dtype)


def _in_proj(x, mod, layer, norm_g, w_in, *, tm, prompt, stacked=None):
    groups, rows, d = x.shape
    mod_rows = mod.shape[2]
    widths = [hi - lo for lo, hi in _stream_columns(d)]
    inter = BF16 if prompt else F32
    dtypes = [F32, inter, inter, F32, F32, inter, inter]
    tile = lambda w: pl.BlockSpec((None, tm, w), lambda g, i: (g, i, 0))
    tile_t = lambda w: pl.BlockSpec((None, w, tm), lambda g, i: (g, 0, i))
    out_shapes = [jax.ShapeDtypeStruct((groups, rows, w), t) for w, t in zip(widths, dtypes)]
    out_specs = [tile(w) for w in widths]
    in_specs = [
        tile(d),
        _mod_spec(layer, 0, mod_rows, d),
        _mod_spec(layer, 1, mod_rows, d),
        pl.BlockSpec((None, 1, d), lambda g, i: (layer, 0, 0)),
        pl.BlockSpec((None, d, w_in.shape[2]), lambda g, i: (layer, 0, 0)),
    ]
    args = [x, mod, mod, norm_g, w_in]
    aliases = {}
    if prompt:
        depth = w_in.shape[0]
        out_shapes[2] = jax.ShapeDtypeStruct((groups, widths[2], rows), BF16)
        out_specs[2] = tile_t(widths[2])
        out_shapes[3] = jax.ShapeDtypeStruct((depth, groups, widths[3], rows), F32)
        out_specs[3] = pl.BlockSpec((None, None, widths[3], tm), lambda g, i: (layer, g, 0, i))
        out_shapes[4] = jax.ShapeDtypeStruct((depth, groups, rows * N_DH, V_DIM), F32)
        out_specs[4] = pl.BlockSpec((None, None, tm * N_DH, V_DIM), lambda g, i: (layer, g, i, 0))
        out_shapes += [jax.ShapeDtypeStruct((groups, rows, widths[3]), BF16),
                       jax.ShapeDtypeStruct((groups, widths[4], rows), BF16)]
        out_specs += [tile(widths[3]), tile_t(widths[4])]
        if stacked is not None:
            aliases = {len(args): 3, len(args) + 1: 4}
            in_specs += [pl.BlockSpec(memory_space=pl.ANY)] * 2
            args += list(stacked)
    return pl.pallas_call(
        functools.partial(_in_proj_kernel, prompt=prompt),
        grid=(groups, rows // tm),
        in_specs=in_specs,
        out_specs=out_specs,
        out_shape=out_shapes,
        input_output_aliases=aliases,
        compiler_params=_params(("parallel", "parallel")),
    )(*args)


def _sigmoid(z):
    return 0.5 * jnp.tanh(0.5 * z) + 0.5


def _softplus(z):
    return jnp.maximum(z, 0.0) + jnp.log1p(jnp.exp(-jnp.abs(z)))


def _block_diag_dot(xb, w_ref):
    blk = xb.shape[1] // N_RNN_BLOCKS
    return jnp.concatenate(
        [jnp.dot(xb[:, n * blk:(n + 1) * blk], w_ref[n], preferred_element_type=F32)
         for n in range(N_RNN_BLOCKS)], axis=1)


def _lru_terms(xc, wa_ref, ba_ref, wx_ref, bx_ref, lam_ref):
    xb = xc.astype(BF16)
    r = _sigmoid(_block_diag_dot(xb, wa_ref) + ba_ref[...])
    i = _sigmoid(_block_diag_dot(xb, wx_ref) + bx_ref[...])
    log_a = -LRU_C * r * _softplus(-lam_ref[...])
    a = jnp.exp(log_a)
    u = -jnp.tanh(log_a) * (a * a + 1.0)
    b = jnp.where(u > 0.0, u * lax.rsqrt(u), 0.0) * (i * xc)
    return a, b


def _gelu_tanh(y):
    return 0.5 * y * (1.0 + jnp.tanh(math.sqrt(2.0 / math.pi) * (y + 0.044715 * (y * y * y))))


def _rnn_prompt_kernel(xr_ref, yr_ref, cw_ref, cb_ref, wa_ref, ba_ref, wx_ref, bx_ref, lam_ref,
                       out_ref, hlast_ref, conv_ref, xbuf, h_scr):
    t = pl.program_id(1)
    steps = xr_ref.shape[0]
    hist = SUBLANES

    @pl.when(t == 0)
    def _():
        xbuf[0:hist, :] = jnp.zeros((hist, xbuf.shape[1]), F32)
        h_scr[...] = jnp.zeros_like(h_scr)

    x = xr_ref[...]
    xbuf[hist:hist + steps, :] = x
    xc = cb_ref[...] + cw_ref[CONV_WIDTH - 1:CONV_WIDTH, :] * x
    for j in range(CONV_WIDTH - 1):
        back = CONV_WIDTH - 1 - j
        xc = xc + cw_ref[j:j + 1, :] * xbuf[hist - back:hist - back + steps, :]
    xbuf[0:hist, :] = x[steps - hist:steps, :]

    a, b = _lru_terms(xc, wa_ref, ba_ref, wx_ref, bx_ref, lam_ref)
    groups = steps // SUBLANES
    a = a.reshape(groups, SUBLANES, a.shape[1])
    b = b.reshape(groups, SUBLANES, b.shape[1])
    sub = lax.broadcasted_iota(jnp.int32, (1, SUBLANES, 1), 1)
    shift = 1
    while shift < SUBLANES:
        a_prev = pltpu.roll(a, shift, axis=1)
        b_prev = pltpu.roll(b, shift, axis=1)
        valid = sub >= shift
        b = jnp.where(valid, a * b_prev + b, b)
        a = jnp.where(valid, a * a_prev, a)
        shift *= 2
    h_prev = h_scr[...]
    hs = []
    for g in range(groups):
        h_g = a[g] * h_prev + b[g]
        hs.append(h_g)
        h_prev = h_g[SUBLANES - 1:SUBLANES, :]
    h = jnp.concatenate(hs, axis=0)
    h_scr[...] = h_prev
    out_ref[...] = (h * _gelu_tanh(yr_ref[...].astype(F32))).astype(out_ref.dtype)

    @pl.when(t == pl.num_programs(1) - 1)
    def _():
        hlast_ref[...] = h[steps - 1:steps, :]
        conv_ref[...] = x[steps - (CONV_WIDTH - 1):steps, :]


def _rnn_weight_specs(layer, c):
    idx2 = lambda *_: (layer, 0, 0)
    idx3 = lambda *_: (layer, 0, 0, 0)
    blk = c // N_RNN_BLOCKS
    return [
        pl.BlockSpec((None, CONV_WIDTH, c), idx2),
        pl.BlockSpec((None, 1, c), idx2),
        pl.BlockSpec((None, N_RNN_BLOCKS, blk, blk), idx3),
        pl.BlockSpec((None, 1, c), idx2),
        pl.BlockSpec((None, N_RNN_BLOCKS, blk, blk), idx3),
        pl.BlockSpec((None, 1, c), idx2),
        pl.BlockSpec((None, 1, c), idx2),
    ]


def _rnn_prompt(xr, yr, layer, rnn_w):
    batch, seq, c = xr.shape
    tile = pl.BlockSpec((None, RNN_CHUNK, c), lambda b, t: (b, t, 0))
    return pl.pallas_call(
        _rnn_prompt_kernel,
        grid=(batch, seq // RNN_CHUNK),
        in_specs=[tile, tile] + _rnn_weight_specs(layer, c),
        out_specs=[
            tile,
            pl.BlockSpec((None, 1, c), lambda b, t: (b, 0, 0)),
            pl.BlockSpec((None, CONV_WIDTH - 1, c), lambda b, t: (b, 0, 0)),
        ],
        out_shape=[
            jax.ShapeDtypeStruct((batch, seq, c), BF16),
            jax.ShapeDtypeStruct((batch, 1, c), F32),
            jax.ShapeDtypeStruct((batch, CONV_WIDTH - 1, c), F32),
        ],
        scratch_shapes=[pltpu.VMEM((SUBLANES + RNN_CHUNK, c), F32), pltpu.VMEM((1, c), F32)],
        compiler_params=_params(("parallel", "arbitrary")),
    )(xr, yr, *rnn_w)


def _rnn_sample_kernel(xr_ref, yr_ref, hist_ref, h0_ref, cw_ref, cb_ref, wa_ref, ba_ref, wx_ref,
                       bx_ref, lam_ref, out_ref, hnew_ref, conv_ref):
    x = xr_ref[...]
    xc = cb_ref[...] + cw_ref[CONV_WIDTH - 1:CONV_WIDTH, :] * x
    for j in range(CONV_WIDTH - 1):
        xc = xc + cw_ref[j:j + 1, :] * hist_ref[j]
    a, b = _lru_terms(xc, wa_ref, ba_ref, wx_ref, bx_ref, lam_ref)
    h = a * h0_ref[...] + b
    hnew_ref[...] = h
    out_ref[...] = h * _gelu_tanh(yr_ref[...])
    for j in range(CONV_WIDTH - 2):
        conv_ref[j] = hist_ref[j + 1]
    conv_ref[CONV_WIDTH - 2] = x


def _rnn_sample(xr, yr, hist_t, h0, layer, rnn_w):
    n, c = xr.shape
    full = pl.BlockSpec((n, c), lambda i: (0, 0))
    hist_spec = pl.BlockSpec((None, CONV_WIDTH - 1, n, c), lambda i: (layer, 0, 0, 0))
    return pl.pallas_call(
        _rnn_sample_kernel,
        grid=(1,),
        in_specs=[full, full, hist_spec, pl.BlockSpec((None, n, c), lambda i: (layer, 0, 0))]
        + _rnn_weight_specs(layer, c),
        out_specs=[full, full, pl.BlockSpec((CONV_WIDTH - 1, n, c), lambda i: (0, 0, 0))],
        out_shape=[
            jax.ShapeDtypeStruct((n, c), F32),
            jax.ShapeDtypeStruct((n, c), F32),
            jax.ShapeDtypeStruct((CONV_WIDTH - 1, n, c), F32),
        ],
        compiler_params=_params(("arbitrary",)),
    )(xr, yr, hist_t, h0, *rnn_w)


def _attn_prompt_kernel(q_ref, k_ref, v_ref, lq1_ref, lk1_ref, lq2_ref, lk2_ref, g_ref, o_ref,
                        m_scr, l_scr, acc_scr, s_scr, *, lam_init):
    i = pl.program_id(2)
    tq = q_ref.shape[1]
    tk = tq
    feats = 2 * HEAD_DIM
    heads = q_ref.shape[0] // feats
    feat = lax.broadcasted_iota(jnp.int32, (feats, tq), 0)
    qq = []
    for h in range(heads):
        qt = q_ref[h * feats:(h + 1) * feats, :]
        zero = jnp.zeros_like(qt)
        qq.append(jnp.concatenate([jnp.where(feat < HEAD_DIM, qt, zero),
                                   jnp.where(feat >= HEAD_DIM, qt, zero)], axis=1))
    m_scr[...] = jnp.full(m_scr.shape, MASK_VALUE, F32)
    l_scr[...] = jnp.zeros_like(l_scr)
    acc_scr[...] = jnp.zeros_like(acc_scr)

    def scores(h, j, slot):
        start = pl.multiple_of(j * tk, tk)
        kj = k_ref[pl.ds(start, tk), h * feats:(h + 1) * feats]
        s_scr[h, slot] = jnp.dot(kj, qq[h], preferred_element_type=F32)

    def update(h, j, slot, masked):
        start = pl.multiple_of(j * tk, tk)
        vj = v_ref[h * V_DIM:(h + 1) * V_DIM, pl.ds(start, tk)]
        s = s_scr[h, slot]
        if masked:
            kpos = lax.broadcasted_iota(jnp.int32, s.shape, 0)
            c = lax.broadcasted_iota(jnp.int32, s.shape, 1)
            qpos = jnp.where(c >= tq, c - tq, c)
            s = jnp.where(kpos <= qpos, s, MASK_VALUE)
        m_prev = m_scr[h]
        m_next = jnp.maximum(m_prev, jnp.max(s, axis=0, keepdims=True))
        alpha = jnp.exp2(m_prev - m_next)
        p = jnp.exp2(s - m_next)
        l_scr[h] = alpha * l_scr[h] + jnp.sum(p, axis=0, keepdims=True)
        acc_scr[h] = alpha * acc_scr[h] + jnp.dot(vj, p.astype(BF16), preferred_element_type=F32)
        m_scr[h] = m_next

    def advance(j, slot, masked):
        for h in range(heads):
            if not masked:
                scores(h, j + 1, 1 - slot)
            other = (h + 1) % heads
            update(other, j, slot, masked)

    def group(g, carry):
        for u in range(ATTN_UNROLL):
            advance(g * ATTN_UNROLL + u, u % 2, False)
        return carry

    for h in range(heads):
        scores(h, 0, 0)
    n_groups = i // ATTN_UNROLL
    lax.fori_loop(0, n_groups, group, 0)
    done = n_groups * ATTN_UNROLL
    width = ATTN_UNROLL // 2
    while width >= 2:
        @pl.when((i & width) != 0)
        def _(done=done, width=width):
            for u in range(width):
                advance(done + u, u % 2, False)
        done = done + (i & width)
        width //= 2

    @pl.when((i & 1) != 0)
    def _():
        advance(done, 0, False)
        advance(done + 1, 1, True)

    @pl.when((i & 1) == 0)
    def _():
        advance(done, 0, True)

    lam = _lam_value(lq1_ref, lk1_ref, lq2_ref, lk2_ref, lam_init)
    for h in range(heads):
        o = acc_scr[h] / l_scr[h]
        o = (o[:, :tq] - lam * o[:, tq:]).T
        o_ref[:, h * V_DIM:(h + 1) * V_DIM] = (
            _rms_norm(o, g_ref[...]) * (1.0 - lam_init)).astype(o_ref.dtype)


def _lam_specs(layer):
    idx = lambda *_: (layer, 0, 0)
    return [pl.BlockSpec((None, 1, HEAD_DIM), idx)] * 4 + [pl.BlockSpec((None, 1, V_DIM), idx)]


def _attn_prompt(q, k, v, layer, lam_w, lam_init):
    batch, seq, _ = k.shape
    tq = ATTN_TILE
    hs = ATTN_HEADS
    q_spec = pl.BlockSpec((None, hs * 2 * HEAD_DIM, tq), lambda b, h, i: (b, h, i))
    k_spec = pl.BlockSpec((None, seq, hs * 2 * HEAD_DIM), lambda b, h, i: (b, 0, h))
    v_spec = pl.BlockSpec((None, hs * V_DIM, seq), lambda b, h, i: (b, h, 0))
    return pl.pallas_call(
        functools.partial(_attn_prompt_kernel, lam_init=lam_init),
        grid=(batch, N_DH // hs, seq // tq),
        in_specs=[q_spec, k_spec, v_spec] + _lam_specs(layer),
        out_specs=pl.BlockSpec((None, tq, hs * V_DIM), lambda b, h, i: (b, i, h)),
        out_shape=jax.ShapeDtypeStruct((batch, seq, N_DH * V_DIM), BF16),
        scratch_shapes=[pltpu.VMEM((hs, 1, 2 * tq), F32), pltpu.VMEM((hs, 1, 2 * tq), F32),
                        pltpu.VMEM((hs, V_DIM, 2 * tq), F32), pltpu.VMEM((hs, 2, tq, 2 * tq), F32)],
        compiler_params=_params(("parallel", "parallel", "arbitrary")),
    )(q, k, v, *lam_w)


def _decode_reset(m_scr, l_scr, acc_scr):
    m_scr[...] = jnp.full(m_scr.shape, MASK_VALUE, F32)
    l_scr[...] = jnp.zeros_like(l_scr)
    acc_scr[...] = jnp.zeros_like(acc_scr)


def _decode_query(q_ref):
    width = q_ref.shape[1]
    map_row = lax.broadcasted_iota(jnp.int32, (N_MAPS, width), 0)
    lane = lax.broadcasted_iota(jnp.int32, (N_MAPS, width), 1)
    qbd = jnp.where(lane // HEAD_DIM == map_row, jnp.broadcast_to(q_ref[...], (N_MAPS, width)), 0.0)
    return qbd


def _decode_scores(qbd, k_ref):
    qb = qbd.astype(BF16)
    return jnp.concatenate(
        [jnp.dot(qb, k_ref[i].astype(BF16), preferred_element_type=F32) for i in range(k_ref.shape[0])],
        axis=1)


def _decode_update(s, v_ref, m_scr, l_scr, acc_scr):
    pages = v_ref.shape[0]
    rows = PAGE_SIZE * N_DH
    m_prev = m_scr[...]
    m_next = jnp.maximum(m_prev, jnp.max(s, axis=1, keepdims=True))
    alpha = jnp.exp(m_prev - m_next)
    p = jnp.exp(s - _repeat(m_next, pages, axis=1))
    l_scr[...] = alpha * l_scr[...] + jnp.sum(p, axis=1, keepdims=True)
    p_rows = jnp.concatenate([p[:, i * PAGE_SIZE:(i + 1) * PAGE_SIZE] for i in range(pages)],
                             axis=0).astype(BF16)
    token = lax.broadcasted_iota(jnp.int32, (PAGE_SIZE, rows), 0)
    row = lax.broadcasted_iota(jnp.int32, (PAGE_SIZE, rows), 1)
    expand = jnp.where(row // N_DH == token, 1.0, 0.0).astype(BF16)
    spread = jnp.dot(p_rows, expand, preferred_element_type=F32)
    r = lax.broadcasted_iota(jnp.int32, spread.shape, 0)
    c = lax.broadcasted_iota(jnp.int32, spread.shape, 1)
    spread = jnp.where(c % N_DH == (r % N_MAPS) // 2, spread, 0.0)
    weights = jnp.concatenate([spread[i * N_MAPS:(i + 1) * N_MAPS, :] for i in range(pages)],
                              axis=1).astype(BF16)
    values = v_ref[...].reshape(pages * rows, V_DIM).astype(BF16)
    acc_scr[...] = alpha * acc_scr[...] + jnp.dot(weights, values, preferred_element_type=F32)
    m_scr[...] = m_next


def _decode_finish(qbd, kn_ref, vn_ref, lam_refs, g_ref, m_scr, l_scr, acc_scr, lam_init):
    s_new = jnp.sum(qbd * kn_ref[...], axis=1, keepdims=True)
    m_prev = m_scr[...]
    m_fin = jnp.maximum(m_prev, s_new)
    alpha = jnp.exp(m_prev - m_fin)
    p_new = jnp.exp(s_new - m_fin)
    l_fin = alpha * l_scr[...] + p_new
    vn = vn_ref[...]
    vn_rows = jnp.concatenate([vn[:, (j // 2) * V_DIM:(j // 2 + 1) * V_DIM] for j in range(N_MAPS)],
                              axis=0)
    res = (alpha * acc_scr[...] + p_new * vn_rows) / l_fin
    lam = _lam_value(*lam_refs, lam_init)
    g = g_ref[...]
    return jnp.concatenate(
        [_rms_norm(res[2 * h:2 * h + 1, :] - lam * res[2 * h + 1:2 * h + 2, :], g) for h in range(N_DH)],
        axis=1) * (1.0 - lam_init)


def _merge_kernel(x_ref, rnn_ref, att_ref, gr_ref, ga_ref, g1_ref, wpr_ref, wpa_ref, wout_ref, o_ref):
    pr = jnp.dot(rnn_ref[...].astype(BF16), wpr_ref[...], preferred_element_type=F32)
    pa = jnp.dot(att_ref[...].astype(BF16), wpa_ref[...], preferred_element_type=F32)
    merged = (jax.nn.sigmoid(gr_ref[...].astype(F32)) * pr
              + jax.nn.sigmoid(ga_ref[...].astype(F32)) * pa)
    out = jnp.dot(merged.astype(BF16), wout_ref[...], preferred_element_type=F32)
    o_ref[...] = x_ref[...] + g1_ref[...] * out


def _merge(x, rnn_out, att, gr, ga, mod, layer, w_pr, w_pa, w_out, *, tm):
    groups, rows, d = x.shape
    tile = lambda a: pl.BlockSpec((None, tm, a.shape[2]), lambda g, i: (g, i, 0))
    weight = lambda w: pl.BlockSpec((None,) + w.shape[1:], lambda g, i: (layer, 0, 0))
    return pl.pallas_call(
        _merge_kernel,
        grid=(groups, rows // tm),
        in_specs=[tile(x), tile(rnn_out), tile(att), tile(gr), tile(ga),
                  _mod_spec(layer, 2, mod.shape[2], d), weight(w_pr), weight(w_pa), weight(w_out)],
        out_specs=tile(x),
        out_shape=jax.ShapeDtypeStruct(x.shape, F32),
        compiler_params=_params(("parallel", "parallel")),
    )(x, rnn_out, att, gr, ga, mod, w_pr, w_pa, w_out)


def _mlp_hidden(x, sh_ref, sc_ref, ng_ref):
    return (_rms_norm(x, ng_ref[...]) * (1.0 + sc_ref[...]) + sh_ref[...]).astype(BF16)


def _ff_up(hb, wup_ref, lo):
    u = jnp.maximum(jnp.dot(hb, wup_ref[:, lo:lo + FF_CHUNK], preferred_element_type=F32), 0.0)
    return (u * u).astype(BF16)


def _ff_down(u, wdn_ref, lo):
    return jnp.dot(u, wdn_ref[lo:lo + FF_CHUNK, :], preferred_element_type=F32)


def _ff_chunk(hb, wup_ref, wdn_ref, lo):
    return _ff_down(_ff_up(hb, wup_ref, lo), wdn_ref, lo)


def _mlp_finish(x, acc, g2_ref, fg_ref, o_ref):
    y = x + g2_ref[...] * acc
    o_ref[...] = y if fg_ref is None else _rms_norm(y, fg_ref[...])


def _mlp_kernel(x_ref, sh_ref, sc_ref, g2_ref, ng_ref, wup_ref, wdn_ref, *rest, final):
    fg_ref = rest[0] if final else None
    o_ref = rest[-1]
    x = x_ref[...]
    hb = _mlp_hidden(x, sh_ref, sc_ref, ng_ref)
    acc = jnp.zeros(x.shape, F32)
    for lo in range(0, wup_ref.shape[1], FF_CHUNK):
        acc = acc + _ff_chunk(hb, wup_ref, wdn_ref, lo)
    _mlp_finish(x, acc, g2_ref, fg_ref, o_ref)


def _mlp_decode_kernel(pt_ref, x_ref, sh_ref, sc_ref, g2_ref, ng_ref, wup_ref, wdn_ref, *rest,
                       final, lam_init, layer, n_pages):
    fg_ref = rest[0] if final else None
    (q_ref, kn_ref, vn_ref, lq1_ref, lk1_ref, lq2_ref, lk2_ref, sg_ref, ck_ref, cv_ref,
     o_ref, os_ref, kbuf, vbuf, sem, m_scr, l_scr, acc_scr) = rest[1 if final else 0:]
    steps = pl.num_programs(1)
    t = pl.program_id(0) * steps + pl.program_id(1)
    total = pl.num_programs(0) * steps
    n_chunks = n_pages // DECODE_PAGES
    n_ff = wup_ref.shape[1] // FF_CHUNK

    def copies(seq, chunk, slot):
        out = []
        for p in range(DECODE_PAGES):
            page = pt_ref[seq * n_pages + chunk * DECODE_PAGES + p]
            out.append(pltpu.make_async_copy(ck_ref.at[layer, page], kbuf.at[slot, p], sem.at[slot, 0]))
            out.append(pltpu.make_async_copy(cv_ref.at[layer, page], vbuf.at[slot, p], sem.at[slot, 1]))
        return out

    def start(seq, chunk, slot):
        for copy in copies(seq, chunk, slot):
            copy.start()

    def wait(seq, chunk, slot):
        for copy in copies(seq, chunk, slot):
            copy.wait()

    @pl.when(t == 0)
    def _():
        start(0, 0, 0)
        start(0, 1, 1)

    qbd = _decode_query(q_ref)
    _decode_reset(m_scr, l_scr, acc_scr)
    x = x_ref[...]
    hb = _mlp_hidden(x, sh_ref, sc_ref, ng_ref)
    acc = jnp.zeros(x.shape, F32)
    u = None
    n_pieces = 2 * n_ff
    for c in range(n_chunks):
        slot = c % 2
        wait(t, c, slot)
        s = _decode_scores(qbd, kbuf.at[slot])
        for piece in range(c * n_pieces // n_chunks, (c + 1) * n_pieces // n_chunks):
            lo = (piece // 2) * FF_CHUNK
            if piece % 2 == 0:
                u = _ff_up(hb, wup_ref, lo)
            else:
                acc = acc + _ff_down(u, wdn_ref, lo)
        _decode_update(s, vbuf.at[slot], m_scr, l_scr, acc_scr)
        if c + 2 < n_chunks:
            start(t, c + 2, slot)
        else:
            @pl.when(t + 1 < total)
            def _(c=c, slot=slot):
                start(t + 1, c + 2 - n_chunks, slot)
    os_ref[...] = _decode_finish(qbd, kn_ref, vn_ref,
                                 (lq1_ref, lk1_ref, lq2_ref, lk2_ref), sg_ref,
                                 m_scr, l_scr, acc_scr, lam_init)
    _mlp_finish(x, acc, g2_ref, fg_ref, o_ref)


def _mlp(x, mod, layer, norm_g, w_up, w_down, final_g, *, tm, final):
    groups, rows, d = x.shape
    tile = pl.BlockSpec((None, tm, d), lambda g, i: (g, i, 0))
    weight = lambda w: pl.BlockSpec((None,) + w.shape[1:], lambda g, i: (layer, 0, 0))
    mod_rows = mod.shape[2]
    in_specs = [tile, _mod_spec(layer, 3, mod_rows, d), _mod_spec(layer, 4, mod_rows, d),
                _mod_spec(layer, 5, mod_rows, d), pl.BlockSpec((None, 1, d), lambda g, i: (layer, 0, 0)),
                weight(w_up), weight(w_down)]
    args = [x, mod, mod, mod, norm_g, w_up, w_down]
    if final:
        in_specs.append(pl.BlockSpec((1, d), lambda g, i: (0, 0)))
        args.append(final_g)
    return pl.pallas_call(
        functools.partial(_mlp_kernel, final=final),
        grid=(groups, rows // tm),
        in_specs=in_specs,
        out_specs=tile,
        out_shape=jax.ShapeDtypeStruct(x.shape, F32),
        compiler_params=_params(("parallel", "parallel")),
    )(*args)


def _mlp_decode(x, mod, layer, norm_g, w_up, w_down, final_g, q, k_new, v_new, cache_k, cache_v,
                page_table, lam_w, lam_init, *, tm, final):
    groups, rows, d = x.shape
    n, _, width = q.shape
    n_pages = page_table.shape[1]
    steps = rows // tm
    assert n == groups * steps, "one running sequence per MLP grid step"
    assert n_pages % (2 * DECODE_PAGES) == 0, "page chunks alternate between two slots"
    mod_rows = mod.shape[2]
    tile = pl.BlockSpec((None, tm, d), lambda g, i, pt: (g, i, 0))
    mod_spec = lambda which: pl.BlockSpec((None, None, mod_rows, d), lambda g, i, pt: (layer, g, 0, which))
    per_layer = lambda shape, **kw: pl.BlockSpec((None,) + shape, lambda g, i, pt: (layer,) + (0,) * len(shape), **kw)
    row = pl.BlockSpec((None, 1, width), lambda g, i, pt: (g * steps + i, 0, 0))
    hbm = pl.BlockSpec(memory_space=pl.ANY)
    in_specs = [tile, mod_spec(3), mod_spec(4), mod_spec(5), per_layer((1, d)),
                per_layer(w_up.shape[1:], pipeline_mode=pl.Buffered(1)),
                per_layer(w_down.shape[1:], pipeline_mode=pl.Buffered(1))]
    args = [x, mod, mod, mod, norm_g, w_up, w_down]
    if final:
        in_specs.append(pl.BlockSpec((1, d), lambda g, i, pt: (0, 0)))
        args.append(final_g)
    in_specs += [row, row, row] + [per_layer((1, HEAD_DIM))] * 4 + [per_layer((1, V_DIM)), hbm, hbm]
    args += [q, k_new, v_new, *lam_w, cache_k, cache_v]
    grid_spec = pltpu.PrefetchScalarGridSpec(
        num_scalar_prefetch=1,
        grid=(groups, steps),
        in_specs=in_specs,
        out_specs=[tile, row],
        scratch_shapes=[pltpu.VMEM((2, DECODE_PAGES, width, PAGE_SIZE), F32),
                        pltpu.VMEM((2, DECODE_PAGES, PAGE_SIZE * N_DH, V_DIM), F32),
                        pltpu.SemaphoreType.DMA((2, 2)),
                        pltpu.VMEM((N_MAPS, LANES), F32), pltpu.VMEM((N_MAPS, LANES), F32),
                        pltpu.VMEM((N_MAPS, V_DIM), F32)],
    )
    return pl.pallas_call(
        functools.partial(_mlp_decode_kernel, final=final, lam_init=lam_init, layer=layer,
                          n_pages=n_pages),
        grid_spec=grid_spec,
        out_shape=[jax.ShapeDtypeStruct(x.shape, F32), jax.ShapeDtypeStruct((n, 1, width), F32)],
        compiler_params=_params(("arbitrary", "arbitrary")),
    )(page_table.reshape(-1), *args)


def kernel(x_prompt, x_sample, c_prompt, c_sample, cache_k, cache_v, state_rnn, state_conv, page_table, w_ada, b_ada, norm1_g, norm2_g, w_in, conv_w, conv_b, w_a, b_a, w_x, b_x, lru_lambda, lq1, lk1, lq2, lk2, subln_g, w_pr, w_pa, w_out, w_up, w_down, final_g):
    depth, d = norm1_g.shape
    batch, seq, _ = x_prompt.shape
    n_dec = x_sample.shape[0]
    d_att = N_DH * 2 * HEAD_DIM
    assert x_sample.shape[1] == 1, "one new token per running sequence"
    assert seq % ROW_TILE == 0 and seq % RNN_CHUNK == 0 and seq % ATTN_TILE == 0
    assert page_table.shape[1] % DECODE_PAGES == 0 and cache_k.shape[2] == PAGE_SIZE

    w_in_b, w_pr_b, w_pa_b, w_out_b, w_up_b, w_down_b, w_a_b, w_x_b = (
        w.astype(BF16) for w in (w_in, w_pr, w_pa, w_out, w_up, w_down, w_a, w_x))
    row3 = lambda a: a.reshape(depth, 1, a.shape[-1])
    rnn_w = (conv_w, row3(conv_b), w_a_b, row3(b_a), w_x_b, row3(b_x), row3(lru_lambda))
    lam_w = (row3(lq1), row3(lk1), row3(lq2), row3(lk2), row3(subln_g))
    norm1, norm2 = row3(norm1_g), row3(norm2_g)
    final_g2 = final_g.reshape(1, d)

    n_cond = n_dec + batch
    pad = -n_cond % (2 * SUBLANES)
    c_all = jnp.concatenate([c_sample, c_prompt, jnp.zeros((pad, d), F32)], axis=0)
    mods = _ada_mod(c_all, w_ada, b_ada)
    mod_s = mods[:, :n_dec].reshape(depth, 1, n_dec, 6 * d)
    mod_p = mods[:, n_dec:n_cond].reshape(depth, batch, 1, 6 * d)

    ck = jnp.transpose(cache_k, (0, 1, 3, 4, 5, 2)).reshape(depth, cache_k.shape[1], d_att, PAGE_SIZE)
    cv = cache_v.reshape(depth, cache_v.shape[1], PAGE_SIZE * N_DH, V_DIM)
    hist_t = jnp.swapaxes(state_conv, 1, 2)

    xp = x_prompt
    xs = x_sample.reshape(1, n_dec, d)
    h_p, cv_p, k_s, v_s, h_s, cv_s = ([] for _ in range(6))
    kv_p = None
    for l in range(depth):
        lam_init = 0.8 - 0.6 * math.exp(-0.3 * l)
        last = l == depth - 1

        xr, yr, qt, k_all, v_all, gr, ga, kb, vtb = _in_proj(
            xp, mod_p, l, norm1, w_in_b, tm=ROW_TILE, prompt=True, stacked=kv_p)
        kv_p = (k_all, v_all)
        rnn_out, h_last, conv_last = _rnn_prompt(xr, yr, l, rnn_w)
        att = _attn_prompt(qt, kb, vtb, l, lam_w, lam_init)
        xp = _merge(xp, rnn_out, att, gr, ga, mod_p, l, w_pr_b, w_pa_b, w_out_b, tm=ROW_TILE)
        h_p.append(h_last[:, 0]); cv_p.append(conv_last)

        xr, yr, q, k, v, gr, ga = _in_proj(xs, mod_s, l, norm1, w_in_b, tm=n_dec, prompt=False)
        rnn_out, h_new, conv_new = _rnn_sample(xr[0], yr[0], hist_t, state_rnn, l, rnn_w)
        as_rows = lambda a: a.reshape(n_dec, 1, a.shape[-1])
        xp, att = _mlp_decode(xp, mod_p, l, norm2, w_up_b, w_down_b, final_g2, as_rows(q), as_rows(k),
                              as_rows(v), ck, cv, page_table, lam_w, lam_init, tm=ROW_TILE, final=last)
        xs = _merge(xs, rnn_out[None], att.reshape(1, n_dec, -1), gr, ga, mod_s, l,
                    w_pr_b, w_pa_b, w_out_b, tm=n_dec)
        xs = _mlp(xs, mod_s, l, norm2, w_up_b, w_down_b, final_g2, tm=n_dec, final=last)
        k_s.append(k[0]); v_s.append(v[0]); h_s.append(h_new); cv_s.append(jnp.swapaxes(conv_new, 0, 1))

    kshape = lambda n, t: (depth, n, t, N_DH, 2, HEAD_DIM)
    vshape = lambda n, t: (depth, n, t, N_DH, V_DIM)
    k_prompt = jnp.transpose(kv_p[0].reshape(depth, batch, N_DH, 2, HEAD_DIM, seq), (0, 1, 5, 2, 3, 4))
    return (xp, xs.reshape(n_dec, 1, d),
            k_prompt, kv_p[1].reshape(vshape(batch, seq)),
            jnp.stack(h_p), jnp.stack(cv_p),
            jnp.stack(k_s).reshape(kshape(n_dec, 1)), jnp.stack(v_s).reshape(vshape(n_dec, 1)),
            jnp.stack(h_s), jnp.stack(cv_s))
```

```python
import functools
import math

import jax
import jax.numpy as jnp
from jax import lax
from jax.experimental import pallas as pl
from jax.experimental.pallas import tpu as pltpu

F32 = jnp.float32
BF16 = jnp.bfloat16

EPS = 1e-6
LRU_C = 8.0
N_RNN_BLOCKS = 8
CONV_WIDTH = 4
N_DH = 4
HEAD_DIM = 64
V_DIM = 2 * HEAD_DIM
N_MAPS = 2 * N_DH
PAGE_SIZE = 128
MASK_VALUE = -1e30
LOG2_E = math.log2(math.e)

V7X_VMEM_BYTES = 64 * 1024 * 1024
VMEM_LIMIT_BYTES = V7X_VMEM_BYTES - 8 * 1024 * 1024
SUBLANES = 8
BF16_SUBLANES = 2 * SUBLANES
LANES = 128

ROW_TILE = 512
RNN_CHUNK = 256
ATTN_TILE = 256
ATTN_UNROLL = 4
ATTN_HEADS = 4
DECODE_PAGES = 16
DECODE_SLOTS = 2
ADA_TILE = 1536
FF_CHUNK = 1024


def _params(semantics):
    return pltpu.CompilerParams(dimension_semantics=semantics, vmem_limit_bytes=VMEM_LIMIT_BYTES)


def _repeat(x, n, axis):
    return jnp.concatenate([x] * n, axis=axis)


def _rms_norm(x, g):
    return x * lax.rsqrt(jnp.mean(x * x, axis=-1, keepdims=True) + EPS) * g


def _lam_value(lq1_ref, lk1_ref, lq2_ref, lk2_ref, lam_init):
    s1 = jnp.sum(lq1_ref[...] * lk1_ref[...], axis=1, keepdims=True)
    s2 = jnp.sum(lq2_ref[...] * lk2_ref[...], axis=1, keepdims=True)
    return jnp.exp(s1) - jnp.exp(s2) + lam_init


def _ada_kernel(c_ref, w_ref, b_ref, o_ref):
    c = c_ref[...]
    a = (c * jax.nn.sigmoid(c)).astype(BF16)
    o_ref[...] = jnp.dot(a, w_ref[...].astype(BF16), preferred_element_type=F32) + b_ref[...]


def _ada_mod(c_all, w_ada, b_ada):
    depth, d, n = w_ada.shape
    rows = c_all.shape[0]
    return pl.pallas_call(
        _ada_kernel,
        grid=(depth, n // ADA_TILE),
        in_specs=[
            pl.BlockSpec((rows, d), lambda l, j: (0, 0)),
            pl.BlockSpec((None, d, ADA_TILE), lambda l, j: (l, 0, j)),
            pl.BlockSpec((None, 1, ADA_TILE), lambda l, j: (l, 0, j)),
        ],
        out_specs=pl.BlockSpec((None, rows, ADA_TILE), lambda l, j: (l, 0, j)),
        out_shape=jax.ShapeDtypeStruct((depth, rows, n), F32),
        compiler_params=_params(("arbitrary", "arbitrary")),
    )(c_all, w_ada, b_ada.reshape(depth, 1, n))


def _mod_spec(layer, which, rows, d):
    return pl.BlockSpec((None, None, rows, d), lambda g, i: (layer, g, 0, which))


def _stream_columns(d):
    widths = [d, d, N_DH * 2 * HEAD_DIM, N_DH * 2 * HEAD_DIM, N_DH * V_DIM, d, d]
    los = [sum(widths[:i]) for i in range(len(widths))]
    return [(lo, lo + w) for lo, w in zip(los, widths)]


def _in_proj_kernel(x_ref, sh_ref, sc_ref, g_ref, w_ref, *refs, prompt):
    hn = _rms_norm(x_ref[...], g_ref[...]) * (1.0 + sc_ref[...]) + sh_ref[...]
    hb = hn.astype(BF16)
    cols = _stream_columns(x_ref.shape[1])
    stream = lambda idx: jnp.dot(hb, w_ref[:, cols[idx][0]:cols[idx][1]], preferred_element_type=F32)
    q_scale = HEAD_DIM ** -0.5 * (LOG2_E if prompt else 1.0)
    q = stream(2) * q_scale
    k = stream(3)
    v = stream(4)
    if prompt:
        xr_ref, yr_ref, q_ref, k_ref, v_ref, gr_ref, ga_ref, kb_ref, vb_ref = refs[-9:]
        q_ref[...] = q.T.astype(BF16)
        k_ref[...] = k.T
        kb_ref[...] = k.astype(BF16)
        for h in range(N_DH):
            v_ref[pl.ds(h, v.shape[0], stride=N_DH), :] = v[:, h * V_DIM:(h + 1) * V_DIM]
        vb_ref[...] = v.T.astype(BF16)
    else:
        xr_ref, yr_ref, q_ref, k_ref, v_ref, gr_ref, ga_ref = refs
        q_ref[...] = q
        k_ref[...] = k
        v_ref[...] = v
    xr_ref[...] = stream(0)
    yr_ref[...] = stream(1).astype(yr_ref.dtype)
    gr_ref[...] = stream(5).astype(gr_ref.dtype)
    ga_ref[...] = stream(6).astype(ga_ref.dtype)


def _in_proj(x, mod, layer, norm_g, w_in, *, tm, prompt, stacked=None):
    groups, rows, d = x.shape
    mod_rows = mod.shape[2]
    widths = [hi - lo for lo, hi in _stream_columns(d)]
    inter = BF16 if prompt else F32
    dtypes = [F32, inter, inter, F32, F32, inter, inter]
    tile = lambda w: pl.BlockSpec((None, tm, w), lambda g, i: (g, i, 0))
    tile_t = lambda w: pl.BlockSpec((None, w, tm), lambda g, i: (g, 0, i))
    out_shapes = [jax.ShapeDtypeStruct((groups, rows, w), t) for w, t in zip(widths, dtypes)]
    out_specs = [tile(w) for w in widths]
    in_specs = [
        tile(d),
        _mod_spec(layer, 0, mod_rows, d),
        _mod_spec(layer, 1, mod_rows, d),
        pl.BlockSpec((None, 1, d), lambda g, i: (layer, 0, 0)),
        pl.BlockSpec((None, d, w_in.shape[2]), lambda g, i: (layer, 0, 0)),
    ]
    args = [x, mod, mod, norm_g, w_in]
    aliases = {}
    if prompt:
        depth = w_in.shape[0]
        out_shapes[2] = jax.ShapeDtypeStruct((groups, widths[2], rows), BF16)
        out_specs[2] = tile_t(widths[2])
        out_shapes[3] = jax.ShapeDtypeStruct((depth, groups, widths[3], rows), F32)
        out_specs[3] = pl.BlockSpec((None, None, widths[3], tm), lambda g, i: (layer, g, 0, i))
        out_shapes[4] = jax.ShapeDtypeStruct((depth, groups, rows * N_DH, V_DIM), F32)
        out_specs[4] = pl.BlockSpec((None, None, tm * N_DH, V_DIM), lambda g, i: (layer, g, i, 0))
        out_shapes += [jax.ShapeDtypeStruct((groups, rows, widths[3]), BF16),
                       jax.ShapeDtypeStruct((groups, widths[4], rows), BF16)]
        out_specs += [tile(widths[3]), tile_t(widths[4])]
        if stacked is not None:
            aliases = {len(args): 3, len(args) + 1: 4}
            in_specs += [pl.BlockSpec(memory_space=pl.ANY)] * 2
            args += list(stacked)
    return pl.pallas_call(
        functools.partial(_in_proj_kernel, prompt=prompt),
        grid=(groups, rows // tm),
        in_specs=in_specs,
        out_specs=out_specs,
        out_shape=out_shapes,
        input_output_aliases=aliases,
        compiler_params=_params(("parallel", "parallel")),
    )(*args)


def _sigmoid(z):
    return 0.5 * jnp.tanh(0.5 * z) + 0.5


def _softplus(z):
    return jnp.maximum(z, 0.0) + jnp.log1p(jnp.exp(-jnp.abs(z)))


def _block_diag_dot(xb, w_ref):
    blk = xb.shape[1] // N_RNN_BLOCKS
    return jnp.concatenate(
        [jnp.dot(xb[:, n * blk:(n + 1) * blk], w_ref[n], preferred_element_type=F32)
         for n in range(N_RNN_BLOCKS)], axis=1)


def _lru_terms(xc, wa_ref, ba_ref, wx_ref, bx_ref, lam_ref):
    xb = xc.astype(BF16)
    r = _sigmoid(_block_diag_dot(xb, wa_ref) + ba_ref[...])
    i = _sigmoid(_block_diag_dot(xb, wx_ref) + bx_ref[...])
    log_a = -LRU_C * r * _softplus(-lam_ref[...])
    a = jnp.exp(log_a)
    u = -jnp.tanh(log_a) * (a * a + 1.0)
    b = jnp.where(u > 0.0, u * lax.rsqrt(u), 0.0) * (i * xc)
    return a, b


def _gelu_tanh(y):
    return 0.5 * y * (1.0 + jnp.tanh(math.sqrt(2.0 / math.pi) * (y + 0.044715 * (y * y * y))))


def _rnn_prompt_kernel(xr_ref, yr_ref, cw_ref, cb_ref, wa_ref, ba_ref, wx_ref, bx_ref, lam_ref,
                       out_ref, hlast_ref, conv_ref, xbuf, h_scr):
    t = pl.program_id(1)
    steps = xr_ref.shape[0]
    hist = SUBLANES

    @pl.when(t == 0)
    def _():
        xbuf[0:hist, :] = jnp.zeros((hist, xbuf.shape[1]), F32)
        h_scr[...] = jnp.zeros_like(h_scr)

    x = xr_ref[...]
    xbuf[hist:hist + steps, :] = x
    xc = cb_ref[...] + cw_ref[CONV_WIDTH - 1:CONV_WIDTH, :] * x
    for j in range(CONV_WIDTH - 1):
        back = CONV_WIDTH - 1 - j
        xc = xc + cw_ref[j:j + 1, :] * xbuf[hist - back:hist - back + steps, :]
    xbuf[0:hist, :] = x[steps - hist:steps, :]

    a, b = _lru_terms(xc, wa_ref, ba_ref, wx_ref, bx_ref, lam_ref)
    groups = steps // SUBLANES
    a = a.reshape(groups, SUBLANES, a.shape[1])
    b = b.reshape(groups, SUBLANES, b.shape[1])
    sub = lax.broadcasted_iota(jnp.int32, (1, SUBLANES, 1), 1)
    shift = 1
    while shift < SUBLANES:
        a_prev = pltpu.roll(a, shift, axis=1)
        b_prev = pltpu.roll(b, shift, axis=1)
        valid = sub >= shift
        b = jnp.where(valid, a * b_prev + b, b)
        a = jnp.where(valid, a * a_prev, a)
        shift *= 2
    h_prev = h_scr[...]
    hs = []
    for g in range(groups):
        h_g = a[g] * h_prev + b[g]
        hs.append(h_g)
        h_prev = h_g[SUBLANES - 1:SUBLANES, :]
    h = jnp.concatenate(hs, axis=0)
    h_scr[...] = h_prev
    out_ref[...] = (h * _gelu_tanh(yr_ref[...].astype(F32))).astype(out_ref.dtype)

    @pl.when(t == pl.num_programs(1) - 1)
    def _():
        hlast_ref[...] = h[steps - 1:steps, :]
        conv_ref[...] = x[steps - (CONV_WIDTH - 1):steps, :]


def _rnn_weight_specs(layer, c):
    idx2 = lambda *_: (layer, 0, 0)
    idx3 = lambda *_: (layer, 0, 0, 0)
    blk = c // N_RNN_BLOCKS
    return [
        pl.BlockSpec((None, CONV_WIDTH, c), idx2),
        pl.BlockSpec((None, 1, c), idx2),
        pl.BlockSpec((None, N_RNN_BLOCKS, blk, blk), idx3),
        pl.BlockSpec((None, 1, c), idx2),
        pl.BlockSpec((None, N_RNN_BLOCKS, blk, blk), idx3),
        pl.BlockSpec((None, 1, c), idx2),
        pl.BlockSpec((None, 1, c), idx2),
    ]


def _rnn_prompt(xr, yr, layer, rnn_w):
    batch, seq, c = xr.shape
    tile = pl.BlockSpec((None, RNN_CHUNK, c), lambda b, t: (b, t, 0))
    return pl.pallas_call(
        _rnn_prompt_kernel,
        grid=(batch, seq // RNN_CHUNK),
        in_specs=[tile, tile] + _rnn_weight_specs(layer, c),
        out_specs=[
            tile,
            pl.BlockSpec((None, 1, c), lambda b, t: (b, 0, 0)),
            pl.BlockSpec((None, CONV_WIDTH - 1, c), lambda b, t: (b, 0, 0)),
        ],
        out_shape=[
            jax.ShapeDtypeStruct((batch, seq, c), BF16),
            jax.ShapeDtypeStruct((batch, 1, c), F32),
            jax.ShapeDtypeStruct((batch, CONV_WIDTH - 1, c), F32),
        ],
        scratch_shapes=[pltpu.VMEM((SUBLANES + RNN_CHUNK, c), F32), pltpu.VMEM((1, c), F32)],
        compiler_params=_params(("parallel", "arbitrary")),
    )(xr, yr, *rnn_w)


def _rnn_sample_kernel(xr_ref, yr_ref, hist_ref, h0_ref, cw_ref, cb_ref, wa_ref, ba_ref, wx_ref,
                       bx_ref, lam_ref, out_ref, hnew_ref, conv_ref):
    x = xr_ref[...]
    xc = cb_ref[...] + cw_ref[CONV_WIDTH - 1:CONV_WIDTH, :] * x
    for j in range(CONV_WIDTH - 1):
        xc = xc + cw_ref[j:j + 1, :] * hist_ref[j]
    a, b = _lru_terms(xc, wa_ref, ba_ref, wx_ref, bx_ref, lam_ref)
    h = a * h0_ref[...] + b
    hnew_ref[...] = h
    out_ref[...] = h * _gelu_tanh(yr_ref[...])
    for j in range(CONV_WIDTH - 2):
        conv_ref[j] = hist_ref[j + 1]
    conv_ref[CONV_WIDTH - 2] = x


def _rnn_sample(xr, yr, hist_t, h0, layer, rnn_w):
    n, c = xr.shape
    full = pl.BlockSpec((n, c), lambda i: (0, 0))
    hist_spec = pl.BlockSpec((None, CONV_WIDTH - 1, n, c), lambda i: (layer, 0, 0, 0))
    return pl.pallas_call(
        _rnn_sample_kernel,
        grid=(1,),
        in_specs=[full, full, hist_spec, pl.BlockSpec((None, n, c), lambda i: (layer, 0, 0))]
        + _rnn_weight_specs(layer, c),
        out_specs=[full, full, pl.BlockSpec((CONV_WIDTH - 1, n, c), lambda i: (0, 0, 0))],
        out_shape=[
            jax.ShapeDtypeStruct((n, c), F32),
            jax.ShapeDtypeStruct((n, c), F32),
            jax.ShapeDtypeStruct((CONV_WIDTH - 1, n, c), F32),
        ],
        compiler_params=_params(("arbitrary",)),
    )(xr, yr, hist_t, h0, *rnn_w)


def _attn_prompt_kernel(q_ref, k_ref, v_ref, lq1_ref, lk1_ref, lq2_ref, lk2_ref, g_ref, o_ref,
                        m_scr, acc_scr, s_scr, *, lam_init):
    i = pl.program_id(2)
    tq = q_ref.shape[1]
    tk = tq
    feats = 2 * HEAD_DIM
    heads = q_ref.shape[0] // feats
    feat = lax.broadcasted_iota(jnp.int32, (feats, tq), 0)
    qq = []
    for h in range(heads):
        qt = q_ref[h * feats:(h + 1) * feats, :]
        zero = jnp.zeros_like(qt)
        qq.append(jnp.concatenate([jnp.where(feat < HEAD_DIM, qt, zero),
                                   jnp.where(feat >= HEAD_DIM, qt, zero)], axis=1))
    m_scr[...] = jnp.full(m_scr.shape, MASK_VALUE, F32)
    acc_scr[...] = jnp.zeros_like(acc_scr)
    ones = jnp.ones((acc_scr.shape[1] - V_DIM, tk), BF16)

    def scores(h, j, slot):
        start = pl.multiple_of(j * tk, tk)
        kj = k_ref[pl.ds(start, tk), h * feats:(h + 1) * feats]
        s_scr[h, slot] = jnp.dot(kj, qq[h], preferred_element_type=F32)

    def update(h, j, slot, masked):
        start = pl.multiple_of(j * tk, tk)
        vj = v_ref[h * V_DIM:(h + 1) * V_DIM, pl.ds(start, tk)]
        s = s_scr[h, slot]
        if masked:
            kpos = lax.broadcasted_iota(jnp.int32, s.shape, 0)
            c = lax.broadcasted_iota(jnp.int32, s.shape, 1)
            qpos = jnp.where(c >= tq, c - tq, c)
            s = jnp.where(kpos <= qpos, s, MASK_VALUE)
        m_prev = m_scr[h]
        m_next = jnp.maximum(m_prev, jnp.max(s, axis=0, keepdims=True))
        alpha = jnp.exp2(m_prev - m_next)
        p = jnp.exp2(s - m_next)
        acc_scr[h] = alpha * acc_scr[h] + jnp.dot(jnp.concatenate([vj, ones], axis=0), p.astype(BF16),
                                                  preferred_element_type=F32)
        m_scr[h] = m_next

    def advance(j, slot, masked):
        for h in range(heads):
            if not masked:
                scores(h, j + 1, 1 - slot)
            other = (h + 1) % heads
            update(other, j, slot, masked)

    def group(g, carry):
        for u in range(ATTN_UNROLL):
            advance(g * ATTN_UNROLL + u, u % 2, False)
        return carry

    for h in range(heads):
        scores(h, 0, 0)
    n_groups = i // ATTN_UNROLL
    lax.fori_loop(0, n_groups, group, 0)
    done = n_groups * ATTN_UNROLL
    width = ATTN_UNROLL // 2
    while width >= 2:
        @pl.when((i & width) != 0)
        def _(done=done, width=width):
            for u in range(width):
                advance(done + u, u % 2, False)
        done = done + (i & width)
        width //= 2

    @pl.when((i & 1) != 0)
    def _():
        advance(done, 0, False)
        advance(done + 1, 1, True)

    @pl.when((i & 1) == 0)
    def _():
        advance(done, 0, True)

    lam = _lam_value(lq1_ref, lk1_ref, lq2_ref, lk2_ref, lam_init)
    for h in range(heads):
        o = acc_scr[h, :V_DIM, :] / acc_scr[h, V_DIM:V_DIM + 1, :]
        o = (o[:, :tq] - lam * o[:, tq:]).T
        o_ref[:, h * V_DIM:(h + 1) * V_DIM] = (
            _rms_norm(o, g_ref[...]) * (1.0 - lam_init)).astype(o_ref.dtype)


def _lam_specs(layer):
    idx = lambda *_: (layer, 0, 0)
    return [pl.BlockSpec((None, 1, HEAD_DIM), idx)] * 4 + [pl.BlockSpec((None, 1, V_DIM), idx)]


def _attn_prompt(q, k, v, layer, lam_w, lam_init):
    batch, seq, _ = k.shape
    tq = ATTN_TILE
    hs = ATTN_HEADS
    q_spec = pl.BlockSpec((None, hs * 2 * HEAD_DIM, tq), lambda b, h, i: (b, h, i))
    k_spec = pl.BlockSpec((None, seq, hs * 2 * HEAD_DIM), lambda b, h, i: (b, 0, h))
    v_spec = pl.BlockSpec((None, hs * V_DIM, seq), lambda b, h, i: (b, h, 0))
    return pl.pallas_call(
        functools.partial(_attn_prompt_kernel, lam_init=lam_init),
        grid=(batch, N_DH // hs, seq // tq),
        in_specs=[q_spec, k_spec, v_spec] + _lam_specs(layer),
        out_specs=pl.BlockSpec((None, tq, hs * V_DIM), lambda b, h, i: (b, i, h)),
        out_shape=jax.ShapeDtypeStruct((batch, seq, N_DH * V_DIM), BF16),
        scratch_shapes=[pltpu.VMEM((hs, 1, 2 * tq), F32),
                        pltpu.VMEM((hs, V_DIM + BF16_SUBLANES, 2 * tq), F32),
                        pltpu.VMEM((hs, 2, tq, 2 * tq), F32)],
        compiler_params=_params(("parallel", "parallel", "arbitrary")),
    )(q, k, v, *lam_w)


def _decode_reset(m_scr, l_scr, acc_scr):
    m_scr[...] = jnp.full(m_scr.shape, MASK_VALUE, F32)
    l_scr[...] = jnp.zeros_like(l_scr)
    acc_scr[...] = jnp.zeros_like(acc_scr)


def _decode_query(q_ref):
    width = q_ref.shape[1]
    map_row = lax.broadcasted_iota(jnp.int32, (N_MAPS, width), 0)
    lane = lax.broadcasted_iota(jnp.int32, (N_MAPS, width), 1)
    qbd = jnp.where(lane // HEAD_DIM == map_row, jnp.broadcast_to(q_ref[...], (N_MAPS, width)), 0.0)
    return qbd


def _decode_scores(qbd, k_ref):
    qb = qbd.astype(BF16)
    return jnp.concatenate(
        [jnp.dot(qb, k_ref[i].astype(BF16), preferred_element_type=F32) for i in range(k_ref.shape[0])],
        axis=1)


def _decode_update(s, v_ref, m_scr, l_scr, acc_scr):
    pages = v_ref.shape[0]
    rows = PAGE_SIZE * N_DH
    m_prev = m_scr[...]
    m_next = jnp.maximum(m_prev, jnp.max(s, axis=1, keepdims=True))
    alpha = jnp.exp(m_prev - m_next)
    p = jnp.exp(s - _repeat(m_next, pages, axis=1))
    l_scr[...] = alpha * l_scr[...] + jnp.sum(p, axis=1, keepdims=True)
    p_rows = jnp.concatenate([p[:, i * PAGE_SIZE:(i + 1) * PAGE_SIZE] for i in range(pages)],
                             axis=0).astype(BF16)
    token = lax.broadcasted_iota(jnp.int32, (PAGE_SIZE, rows), 0)
    row = lax.broadcasted_iota(jnp.int32, (PAGE_SIZE, rows), 1)
    expand = jnp.where(row // N_DH == token, 1.0, 0.0).astype(BF16)
    spread = jnp.dot(p_rows, expand, preferred_element_type=F32)
    r = lax.broadcasted_iota(jnp.int32, spread.shape, 0)
    c = lax.broadcasted_iota(jnp.int32, spread.shape, 1)
    spread = jnp.where(c % N_DH == (r % N_MAPS) // 2, spread, 0.0)
    weights = jnp.concatenate([spread[i * N_MAPS:(i + 1) * N_MAPS, :] for i in range(pages)],
                              axis=1).astype(BF16)
    values = v_ref[...].reshape(pages * rows, V_DIM).astype(BF16)
    acc_scr[...] = alpha * acc_scr[...] + jnp.dot(weights, values, preferred_element_type=F32)
    m_scr[...] = m_next


def _decode_finish(qbd, kn_ref, vn_ref, lam_refs, g_ref, m_scr, l_scr, acc_scr, lam_init):
    s_new = jnp.sum(qbd * kn_ref[...], axis=1, keepdims=True)
    m_prev = m_scr[...]
    m_fin = jnp.maximum(m_prev, s_new)
    alpha = jnp.exp(m_prev - m_fin)
    p_new = jnp.exp(s_new - m_fin)
    l_fin = alpha * l_scr[...] + p_new
    vn = vn_ref[...]
    vn_rows = jnp.concatenate([vn[:, (j // 2) * V_DIM:(j // 2 + 1) * V_DIM] for j in range(N_MAPS)],
                              axis=0)
    res = (alpha * acc_scr[...] + p_new * vn_rows) / l_fin
    lam = _lam_value(*lam_refs, lam_init)
    g = g_ref[...]
    return jnp.concatenate(
        [_rms_norm(res[2 * h:2 * h + 1, :] - lam * res[2 * h + 1:2 * h + 2, :], g) for h in range(N_DH)],
        axis=1) * (1.0 - lam_init)


def _merge_kernel(x_ref, rnn_ref, att_ref, gr_ref, ga_ref, g1_ref, wpr_ref, wpa_ref, wout_ref, o_ref):
    pr = jnp.dot(rnn_ref[...].astype(BF16), wpr_ref[...], preferred_element_type=F32)
    pa = jnp.dot(att_ref[...].astype(BF16), wpa_ref[...], preferred_element_type=F32)
    merged = (jax.nn.sigmoid(gr_ref[...].astype(F32)) * pr
              + jax.nn.sigmoid(ga_ref[...].astype(F32)) * pa)
    out = jnp.dot(merged.astype(BF16), wout_ref[...], preferred_element_type=F32)
    o_ref[...] = x_ref[...] + g1_ref[...] * out


def _merge(x, rnn_out, att, gr, ga, mod, layer, w_pr, w_pa, w_out, *, tm):
    groups, rows, d = x.shape
    tile = lambda a: pl.BlockSpec((None, tm, a.shape[2]), lambda g, i: (g, i, 0))
    weight = lambda w: pl.BlockSpec((None,) + w.shape[1:], lambda g, i: (layer, 0, 0))
    return pl.pallas_call(
        _merge_kernel,
        grid=(groups, rows // tm),
        in_specs=[tile(x), tile(rnn_out), tile(att), tile(gr), tile(ga),
                  _mod_spec(layer, 2, mod.shape[2], d), weight(w_pr), weight(w_pa), weight(w_out)],
        out_specs=tile(x),
        out_shape=jax.ShapeDtypeStruct(x.shape, F32),
        compiler_params=_params(("parallel", "parallel")),
    )(x, rnn_out, att, gr, ga, mod, w_pr, w_pa, w_out)


def _mlp_hidden(x, sh_ref, sc_ref, ng_ref):
    return (_rms_norm(x, ng_ref[...]) * (1.0 + sc_ref[...]) + sh_ref[...]).astype(BF16)


def _ff_up(hb, wup_ref, lo):
    u = jnp.maximum(jnp.dot(hb, wup_ref[:, lo:lo + FF_CHUNK], preferred_element_type=F32), 0.0)
    return (u * u).astype(BF16)


def _ff_down(u, wdn_ref, lo):
    return jnp.dot(u, wdn_ref[lo:lo + FF_CHUNK, :], preferred_element_type=F32)


def _ff_chunk(hb, wup_ref, wdn_ref, lo):
    return _ff_down(_ff_up(hb, wup_ref, lo), wdn_ref, lo)


def _mlp_finish(x, acc, g2_ref, fg_ref, o_ref):
    y = x + g2_ref[...] * acc
    o_ref[...] = y if fg_ref is None else _rms_norm(y, fg_ref[...])


def _mlp_kernel(x_ref, sh_ref, sc_ref, g2_ref, ng_ref, wup_ref, wdn_ref, *rest, final):
    fg_ref = rest[0] if final else None
    o_ref = rest[-1]
    x = x_ref[...]
    hb = _mlp_hidden(x, sh_ref, sc_ref, ng_ref)
    acc = jnp.zeros(x.shape, F32)
    for lo in range(0, wup_ref.shape[1], FF_CHUNK):
        acc = acc + _ff_chunk(hb, wup_ref, wdn_ref, lo)
    _mlp_finish(x, acc, g2_ref, fg_ref, o_ref)


def _mlp_decode_kernel(pt_ref, x_ref, sh_ref, sc_ref, g2_ref, ng_ref, wup_ref, wdn_ref, *rest,
                       final, lam_init, layer, n_pages):
    fg_ref = rest[0] if final else None
    (q_ref, kn_ref, vn_ref, lq1_ref, lk1_ref, lq2_ref, lk2_ref, sg_ref, ck_ref, cv_ref,
     o_ref, os_ref, kbuf, vbuf, sem, m_scr, l_scr, acc_scr) = rest[1 if final else 0:]
    steps = pl.num_programs(1)
    t = pl.program_id(0) * steps + pl.program_id(1)
    total = pl.num_programs(0) * steps
    n_chunks = n_pages // DECODE_PAGES
    n_ff = wup_ref.shape[1] // FF_CHUNK

    def copies(which, seq, chunk, slot):
        src, dst = ((ck_ref, kbuf), (cv_ref, vbuf))[which]
        return [pltpu.make_async_copy(src.at[layer, pt_ref[seq * n_pages + chunk * DECODE_PAGES + p]],
                                      dst.at[slot, p], sem.at[slot, which])
                for p in range(DECODE_PAGES)]

    def start(which, seq, chunk, slot):
        for copy in copies(which, seq, chunk, slot):
            copy.start()

    def wait(which, seq, chunk, slot):
        for copy in copies(which, seq, chunk, slot):
            copy.wait()

    def start_ahead(which, c, slot):
        if c + DECODE_SLOTS < n_chunks:
            start(which, t, c + DECODE_SLOTS, slot)
        else:
            @pl.when(t + 1 < total)
            def _():
                start(which, t + 1, c + DECODE_SLOTS - n_chunks, slot)

    @pl.when(t == 0)
    def _():
        for c in range(DECODE_SLOTS):
            start(0, 0, c, c)
            start(1, 0, c, c)

    qbd = _decode_query(q_ref)
    _decode_reset(m_scr, l_scr, acc_scr)
    x = x_ref[...]
    hb = _mlp_hidden(x, sh_ref, sc_ref, ng_ref)
    acc = jnp.zeros(x.shape, F32)
    u = None
    n_pieces = 2 * n_ff
    for c in range(n_chunks):
        slot = c % DECODE_SLOTS
        wait(0, t, c, slot)
        s = _decode_scores(qbd, kbuf.at[slot])
        start_ahead(0, c, slot)
        for piece in range(c * n_pieces // n_chunks, (c + 1) * n_pieces // n_chunks):
            lo = (piece // 2) * FF_CHUNK
            if piece % 2 == 0:
                u = _ff_up(hb, wup_ref, lo)
            else:
                acc = acc + _ff_down(u, wdn_ref, lo)
        wait(1, t, c, slot)
        _decode_update(s, vbuf.at[slot], m_scr, l_scr, acc_scr)
        start_ahead(1, c, slot)
    os_ref[...] = _decode_finish(qbd, kn_ref, vn_ref,
                                 (lq1_ref, lk1_ref, lq2_ref, lk2_ref), sg_ref,
                                 m_scr, l_scr, acc_scr, lam_init)
    _mlp_finish(x, acc, g2_ref, fg_ref, o_ref)


def _mlp(x, mod, layer, norm_g, w_up, w_down, final_g, *, tm, final):
    groups, rows, d = x.shape
    tile = pl.BlockSpec((None, tm, d), lambda g, i: (g, i, 0))
    weight = lambda w: pl.BlockSpec((None,) + w.shape[1:], lambda g, i: (layer, 0, 0))
    mod_rows = mod.shape[2]
    in_specs = [tile, _mod_spec(layer, 3, mod_rows, d), _mod_spec(layer, 4, mod_rows, d),
                _mod_spec(layer, 5, mod_rows, d), pl.BlockSpec((None, 1, d), lambda g, i: (layer, 0, 0)),
                weight(w_up), weight(w_down)]
    args = [x, mod, mod, mod, norm_g, w_up, w_down]
    if final:
        in_specs.append(pl.BlockSpec((1, d), lambda g, i: (0, 0)))
        args.append(final_g)
    return pl.pallas_call(
        functools.partial(_mlp_kernel, final=final),
        grid=(groups, rows // tm),
        in_specs=in_specs,
        out_specs=tile,
        out_shape=jax.ShapeDtypeStruct(x.shape, F32),
        compiler_params=_params(("parallel", "parallel")),
    )(*args)


def _mlp_decode(x, mod, layer, norm_g, w_up, w_down, final_g, q, k_new, v_new, cache_k, cache_v,
                page_table, lam_w, lam_init, *, tm, final):
    groups, rows, d = x.shape
    n, _, width = q.shape
    n_pages = page_table.shape[1]
    steps = rows // tm
    assert n == groups * steps, "one running sequence per MLP grid step"
    assert n_pages % (DECODE_SLOTS * DECODE_PAGES) == 0, "page chunks rotate through the slots"
    mod_rows = mod.shape[2]
    tile = pl.BlockSpec((None, tm, d), lambda g, i, pt: (g, i, 0))
    mod_spec = lambda which: pl.BlockSpec((None, None, mod_rows, d), lambda g, i, pt: (layer, g, 0, which))
    per_layer = lambda shape, **kw: pl.BlockSpec((None,) + shape, lambda g, i, pt: (layer,) + (0,) * len(shape), **kw)
    row = pl.BlockSpec((None, 1, width), lambda g, i, pt: (g * steps + i, 0, 0))
    hbm = pl.BlockSpec(memory_space=pl.ANY)
    in_specs = [tile, mod_spec(3), mod_spec(4), mod_spec(5), per_layer((1, d)),
                per_layer(w_up.shape[1:], pipeline_mode=pl.Buffered(1)),
                per_layer(w_down.shape[1:], pipeline_mode=pl.Buffered(1))]
    args = [x, mod, mod, mod, norm_g, w_up, w_down]
    if final:
        in_specs.append(pl.BlockSpec((1, d), lambda g, i, pt: (0, 0)))
        args.append(final_g)
    in_specs += [row, row, row] + [per_layer((1, HEAD_DIM))] * 4 + [per_layer((1, V_DIM)), hbm, hbm]
    args += [q, k_new, v_new, *lam_w, cache_k, cache_v]
    grid_spec = pltpu.PrefetchScalarGridSpec(
        num_scalar_prefetch=1,
        grid=(groups, steps),
        in_specs=in_specs,
        out_specs=[tile, row],
        scratch_shapes=[pltpu.VMEM((DECODE_SLOTS, DECODE_PAGES, width, PAGE_SIZE), F32),
                        pltpu.VMEM((DECODE_SLOTS, DECODE_PAGES, PAGE_SIZE * N_DH, V_DIM), F32),
                        pltpu.SemaphoreType.DMA((DECODE_SLOTS, 2)),
                        pltpu.VMEM((N_MAPS, LANES), F32), pltpu.VMEM((N_MAPS, LANES), F32),
                        pltpu.VMEM((N_MAPS, V_DIM), F32)],
    )
    return pl.pallas_call(
        functools.partial(_mlp_decode_kernel, final=final, lam_init=lam_init, layer=layer,
                          n_pages=n_pages),
        grid_spec=grid_spec,
        out_shape=[jax.ShapeDtypeStruct(x.shape, F32), jax.ShapeDtypeStruct((n, 1, width), F32)],
        compiler_params=_params(("arbitrary", "arbitrary")),
    )(page_table.reshape(-1), *args)


def kernel(x_prompt, x_sample, c_prompt, c_sample, cache_k, cache_v, state_rnn, state_conv, page_table, w_ada, b_ada, norm1_g, norm2_g, w_in, conv_w, conv_b, w_a, b_a, w_x, b_x, lru_lambda, lq1, lk1, lq2, lk2, subln_g, w_pr, w_pa, w_out, w_up, w_down, final_g):
    depth, d = norm1_g.shape
    batch, seq, _ = x_prompt.shape
    n_dec = x_sample.shape[0]
    d_att = N_DH * 2 * HEAD_DIM
    assert x_sample.shape[1] == 1, "one new token per running sequence"
    assert seq % ROW_TILE == 0 and seq % RNN_CHUNK == 0 and seq % ATTN_TILE == 0
    assert page_table.shape[1] % DECODE_PAGES == 0 and cache_k.shape[2] == PAGE_SIZE

    w_in_b, w_pr_b, w_pa_b, w_out_b, w_up_b, w_down_b, w_a_b, w_x_b = (
        w.astype(BF16) for w in (w_in, w_pr, w_pa, w_out, w_up, w_down, w_a, w_x))
    row3 = lambda a: a.reshape(depth, 1, a.shape[-1])
    rnn_w = (conv_w, row3(conv_b), w_a_b, row3(b_a), w_x_b, row3(b_x), row3(lru_lambda))
    lam_w = (row3(lq1), row3(lk1), row3(lq2), row3(lk2), row3(subln_g))
    norm1, norm2 = row3(norm1_g), row3(norm2_g)
    final_g2 = final_g.reshape(1, d)

    n_cond = n_dec + batch
    pad = -n_cond % (2 * SUBLANES)
    c_all = jnp.concatenate([c_sample, c_prompt, jnp.zeros((pad, d), F32)], axis=0)
    mods = _ada_mod(c_all, w_ada, b_ada)
    mod_s = mods[:, :n_dec].reshape(depth, 1, n_dec, 6 * d)
    mod_p = mods[:, n_dec:n_cond].reshape(depth, batch, 1, 6 * d)

    ck = jnp.transpose(cache_k, (0, 1, 3, 4, 5, 2)).reshape(depth, cache_k.shape[1], d_att, PAGE_SIZE)
    cv = cache_v.reshape(depth, cache_v.shape[1], PAGE_SIZE * N_DH, V_DIM)
    hist_t = jnp.swapaxes(state_conv, 1, 2)

    xp = x_prompt
    xs = x_sample.reshape(1, n_dec, d)
    h_p, cv_p, k_s, v_s, h_s, cv_s = ([] for _ in range(6))
    kv_p = None
    for l in range(depth):
        lam_init = 0.8 - 0.6 * math.exp(-0.3 * l)
        last = l == depth - 1

        xr, yr, qt, k_all, v_all, gr, ga, kb, vtb = _in_proj(
            xp, mod_p, l, norm1, w_in_b, tm=ROW_TILE, prompt=True, stacked=kv_p)
        kv_p = (k_all, v_all)
        rnn_out, h_last, conv_last = _rnn_prompt(xr, yr, l, rnn_w)
        att = _attn_prompt(qt, kb, vtb, l, lam_w, lam_init)
        xp = _merge(xp, rnn_out, att, gr, ga, mod_p, l, w_pr_b, w_pa_b, w_out_b, tm=ROW_TILE)
        h_p.append(h_last[:, 0]); cv_p.append(conv_last)

        xr, yr, q, k, v, gr, ga = _in_proj(xs, mod_s, l, norm1, w_in_b, tm=n_dec, prompt=False)
        rnn_out, h_new, conv_new = _rnn_sample(xr[0], yr[0], hist_t, state_rnn, l, rnn_w)
        as_rows = lambda a: a.reshape(n_dec, 1, a.shape[-1])
        xp, att = _mlp_decode(xp, mod_p, l, norm2, w_up_b, w_down_b, final_g2, as_rows(q), as_rows(k),
                              as_rows(v), ck, cv, page_table, lam_w, lam_init, tm=ROW_TILE, final=last)
        xs = _merge(xs, rnn_out[None], att.reshape(1, n_dec, -1), gr, ga, mod_s, l,
                    w_pr_b, w_pa_b, w_out_b, tm=n_dec)
        xs = _mlp(xs, mod_s, l, norm2, w_up_b, w_down_b, final_g2, tm=n_dec, final=last)
        k_s.append(k[0]); v_s.append(v[0]); h_s.append(h_new); cv_s.append(jnp.swapaxes(conv_new, 0, 1))

    kshape = lambda n, t: (depth, n, t, N_DH, 2, HEAD_DIM)
    vshape = lambda n, t: (depth, n, t, N_DH, V_DIM)
    k_prompt = jnp.transpose(kv_p[0].reshape(depth, batch, N_DH, 2, HEAD_DIM, seq), (0, 1, 5, 2, 3, 4))
    return (xp, xs.reshape(n_dec, 1, d),
            k_prompt, kv_p[1].reshape(vshape(batch, seq)),
            jnp.stack(h_p), jnp.stack(cv_p),
            jnp.stack(k_s).reshape(kshape(n_dec, 1)), jnp.stack(v_s).reshape(vshape(n_dec, 1)),
            jnp.stack(h_s), jnp.stack(cv_s))
```

```python
import functools
import math

import jax
import jax.numpy as jnp
from jax import lax
from jax.experimental import pallas as pl
from jax.experimental.pallas import tpu as pltpu

F32 = jnp.float32
BF16 = jnp.bfloat16

EPS = 1e-6
LRU_C = 8.0
N_RNN_BLOCKS = 8
CONV_WIDTH = 4
N_DH = 4
HEAD_DIM = 64
V_DIM = 2 * HEAD_DIM
N_MAPS = 2 * N_DH
PAGE_SIZE = 128
MASK_VALUE = -1e30
LOG2_E = math.log2(math.e)

V7X_VMEM_BYTES = 64 * 1024 * 1024
VMEM_LIMIT_BYTES = V7X_VMEM_BYTES - 8 * 1024 * 1024
SUBLANES = 8
BF16_SUBLANES = 2 * SUBLANES
LANES = 128

ROW_TILE = 512
RNN_CHUNK = 256
ATTN_TILE = 256
ATTN_UNROLL = 4
ATTN_HEADS = 4
DECODE_PAGES = 16
DECODE_SLOTS = 2
ADA_TILE = 1536
FF_CHUNK = 1024


def _params(semantics):
    return pltpu.CompilerParams(dimension_semantics=semantics, vmem_limit_bytes=VMEM_LIMIT_BYTES)


def _repeat(x, n, axis):
    return jnp.concatenate([x] * n, axis=axis)


def _rms_norm(x, g):
    return x * lax.rsqrt(jnp.mean(x * x, axis=-1, keepdims=True) + EPS) * g


def _lam_value(lq1_ref, lk1_ref, lq2_ref, lk2_ref, lam_init):
    s1 = jnp.sum(lq1_ref[...] * lk1_ref[...], axis=1, keepdims=True)
    s2 = jnp.sum(lq2_ref[...] * lk2_ref[...], axis=1, keepdims=True)
    return jnp.exp(s1) - jnp.exp(s2) + lam_init


def _ada_kernel(c_ref, w_ref, b_ref, o_ref):
    c = c_ref[...]
    a = (c * jax.nn.sigmoid(c)).astype(BF16)
    o_ref[...] = jnp.dot(a, w_ref[...].astype(BF16), preferred_element_type=F32) + b_ref[...]


def _ada_mod(c_all, w_ada, b_ada):
    depth, d, n = w_ada.shape
    rows = c_all.shape[0]
    return pl.pallas_call(
        _ada_kernel,
        grid=(depth, n // ADA_TILE),
        in_specs=[
            pl.BlockSpec((rows, d), lambda l, j: (0, 0)),
            pl.BlockSpec((None, d, ADA_TILE), lambda l, j: (l, 0, j)),
            pl.BlockSpec((None, 1, ADA_TILE), lambda l, j: (l, 0, j)),
        ],
        out_specs=pl.BlockSpec((None, rows, ADA_TILE), lambda l, j: (l, 0, j)),
        out_shape=jax.ShapeDtypeStruct((depth, rows, n), F32),
        compiler_params=_params(("arbitrary", "arbitrary")),
    )(c_all, w_ada, b_ada.reshape(depth, 1, n))


def _mod_spec(layer, which, rows, d):
    return pl.BlockSpec((None, None, rows, d), lambda g, i: (layer, g, 0, which))


def _stream_columns(d):
    widths = [d, d, N_DH * 2 * HEAD_DIM, N_DH * 2 * HEAD_DIM, N_DH * V_DIM, d, d]
    los = [sum(widths[:i]) for i in range(len(widths))]
    return [(lo, lo + w) for lo, w in zip(los, widths)]


def _in_proj_kernel(x_ref, sh_ref, sc_ref, g_ref, w_ref, *refs, prompt):
    hn = _rms_norm(x_ref[...], g_ref[...]) * (1.0 + sc_ref[...]) + sh_ref[...]
    hb = hn.astype(BF16)
    cols = _stream_columns(x_ref.shape[1])
    stream = lambda idx: jnp.dot(hb, w_ref[:, cols[idx][0]:cols[idx][1]], preferred_element_type=F32)
    q_scale = HEAD_DIM ** -0.5 * (LOG2_E if prompt else 1.0)
    q = stream(2) * q_scale
    k = stream(3)
    v = stream(4)
    if prompt:
        xr_ref, yr_ref, q_ref, k_ref, v_ref, gr_ref, ga_ref, kb_ref, vb_ref = refs[-9:]
        q_ref[...] = q.T.astype(BF16)
        k_ref[...] = k.T
        kb_ref[...] = k.astype(BF16)
        for h in range(N_DH):
            v_ref[pl.ds(h, v.shape[0], stride=N_DH), :] = v[:, h * V_DIM:(h + 1) * V_DIM]
        vb_ref[...] = v.T.astype(BF16)
    else:
        xr_ref, yr_ref, q_ref, k_ref, v_ref, gr_ref, ga_ref = refs
        q_ref[...] = q
        k_ref[...] = k
        v_ref[...] = v
    xr_ref[...] = stream(0)
    yr_ref[...] = stream(1).astype(yr_ref.dtype)
    gr_ref[...] = stream(5).astype(gr_ref.dtype)
    ga_ref[...] = stream(6).astype(ga_ref.dtype)


def _in_proj(x, mod, layer, norm_g, w_in, *, tm, prompt, stacked=None):
    groups, rows, d = x.shape
    mod_rows = mod.shape[2]
    widths = [hi - lo for lo, hi in _stream_columns(d)]
    inter = BF16 if prompt else F32
    dtypes = [F32, inter, inter, F32, F32, inter, inter]
    tile = lambda w: pl.BlockSpec((None, tm, w), lambda g, i: (g, i, 0))
    tile_t = lambda w: pl.BlockSpec((None, w, tm), lambda g, i: (g, 0, i))
    out_shapes = [jax.ShapeDtypeStruct((groups, rows, w), t) for w, t in zip(widths, dtypes)]
    out_specs = [tile(w) for w in widths]
    in_specs = [
        tile(d),
        _mod_spec(layer, 0, mod_rows, d),
        _mod_spec(layer, 1, mod_rows, d),
        pl.BlockSpec((None, 1, d), lambda g, i: (layer, 0, 0)),
        pl.BlockSpec((None, d, w_in.shape[2]), lambda g, i: (layer, 0, 0)),
    ]
    args = [x, mod, mod, norm_g, w_in]
    aliases = {}
    if prompt:
        depth = w_in.shape[0]
        out_shapes[2] = jax.ShapeDtypeStruct((groups, widths[2], rows), BF16)
        out_specs[2] = tile_t(widths[2])
        out_shapes[3] = jax.ShapeDtypeStruct((depth, groups, widths[3], rows), F32)
        out_specs[3] = pl.BlockSpec((None, None, widths[3], tm), lambda g, i: (layer, g, 0, i))
        out_shapes[4] = jax.ShapeDtypeStruct((depth, groups, rows * N_DH, V_DIM), F32)
        out_specs[4] = pl.BlockSpec((None, None, tm * N_DH, V_DIM), lambda g, i: (layer, g, i, 0))
        out_shapes += [jax.ShapeDtypeStruct((groups, rows, widths[3]), BF16),
                       jax.ShapeDtypeStruct((groups, widths[4], rows), BF16)]
        out_specs += [tile(widths[3]), tile_t(widths[4])]
        if stacked is not None:
            aliases = {len(args): 3, len(args) + 1: 4}
            in_specs += [pl.BlockSpec(memory_space=pl.ANY)] * 2
            args += list(stacked)
    return pl.pallas_call(
        functools.partial(_in_proj_kernel, prompt=prompt),
        grid=(groups, rows // tm),
        in_specs=in_specs,
        out_specs=out_specs,
        out_shape=out_shapes,
        input_output_aliases=aliases,
        compiler_params=_params(("parallel", "parallel")),
    )(*args)


def _sigmoid(z):
    return 0.5 * jnp.tanh(0.5 * z) + 0.5


def _softplus(z):
    return jnp.maximum(z, 0.0) + jnp.log1p(jnp.exp(-jnp.abs(z)))


def _block_diag_dot(xb, w_ref):
    blk = xb.shape[1] // N_RNN_BLOCKS
    return jnp.concatenate(
        [jnp.dot(xb[:, n * blk:(n + 1) * blk], w_ref[n], preferred_element_type=F32)
         for n in range(N_RNN_BLOCKS)], axis=1)


def _lru_terms(xc, wa_ref, ba_ref, wx_ref, bx_ref, lam_ref):
    xb = xc.astype(BF16)
    r = _sigmoid(_block_diag_dot(xb, wa_ref) + ba_ref[...])
    i = _sigmoid(_block_diag_dot(xb, wx_ref) + bx_ref[...])
    log_a = -LRU_C * r * _softplus(-lam_ref[...])
    a = jnp.exp(log_a)
    u = -jnp.tanh(log_a) * (a * a + 1.0)
    b = jnp.where(u > 0.0, u * lax.rsqrt(u), 0.0) * (i * xc)
    return a, b


def _gelu_tanh(y):
    return 0.5 * y * (1.0 + jnp.tanh(math.sqrt(2.0 / math.pi) * (y + 0.044715 * (y * y * y))))


def _rnn_reset(xbuf, h_scr):
    xbuf[0:SUBLANES, :] = jnp.zeros((SUBLANES, xbuf.shape[1]), F32)
    h_scr[...] = jnp.zeros_like(h_scr)


def _rnn_chunk(x, y, cw_ref, cb_ref, wa_ref, ba_ref, wx_ref, bx_ref, lam_ref, xbuf, h_scr):
    steps = x.shape[0]
    hist = SUBLANES
    xbuf[hist:hist + steps, :] = x
    xc = cb_ref[...] + cw_ref[CONV_WIDTH - 1:CONV_WIDTH, :] * x
    for j in range(CONV_WIDTH - 1):
        back = CONV_WIDTH - 1 - j
        xc = xc + cw_ref[j:j + 1, :] * xbuf[hist - back:hist - back + steps, :]
    xbuf[0:hist, :] = x[steps - hist:steps, :]

    a, b = _lru_terms(xc, wa_ref, ba_ref, wx_ref, bx_ref, lam_ref)
    groups = steps // SUBLANES
    a = a.reshape(groups, SUBLANES, a.shape[1])
    b = b.reshape(groups, SUBLANES, b.shape[1])
    sub = lax.broadcasted_iota(jnp.int32, (1, SUBLANES, 1), 1)
    shift = 1
    while shift < SUBLANES:
        a_prev = pltpu.roll(a, shift, axis=1)
        b_prev = pltpu.roll(b, shift, axis=1)
        valid = sub >= shift
        b = jnp.where(valid, a * b_prev + b, b)
        a = jnp.where(valid, a * a_prev, a)
        shift *= 2
    h_prev = h_scr[...]
    hs = []
    for g in range(groups):
        h_g = a[g] * h_prev + b[g]
        hs.append(h_g)
        h_prev = h_g[SUBLANES - 1:SUBLANES, :]
    h = jnp.concatenate(hs, axis=0)
    h_scr[...] = h_prev
    return (h * _gelu_tanh(y.astype(F32))).astype(BF16), h_prev, x[steps - (CONV_WIDTH - 1):steps, :]


def _rnn_weight_specs(layer, c):
    idx2 = lambda *_: (layer, 0, 0)
    idx3 = lambda *_: (layer, 0, 0, 0)
    blk = c // N_RNN_BLOCKS
    return [
        pl.BlockSpec((None, CONV_WIDTH, c), idx2),
        pl.BlockSpec((None, 1, c), idx2),
        pl.BlockSpec((None, N_RNN_BLOCKS, blk, blk), idx3),
        pl.BlockSpec((None, 1, c), idx2),
        pl.BlockSpec((None, N_RNN_BLOCKS, blk, blk), idx3),
        pl.BlockSpec((None, 1, c), idx2),
        pl.BlockSpec((None, 1, c), idx2),
    ]


def _rnn_merge_kernel(x_ref, xr_ref, yr_ref, att_ref, gr_ref, ga_ref, g1_ref,
                      cw_ref, cb_ref, wa_ref, ba_ref, wx_ref, bx_ref, lam_ref, wpr_ref, wpa_ref, wout_ref,
                      o_ref, hlast_ref, conv_ref, xbuf, h_scr):
    t = pl.program_id(1)

    @pl.when(t == 0)
    def _():
        _rnn_reset(xbuf, h_scr)

    outs = []
    for r in range(0, xr_ref.shape[0], RNN_CHUNK):
        out, h_last, x_last = _rnn_chunk(xr_ref[r:r + RNN_CHUNK, :], yr_ref[r:r + RNN_CHUNK, :],
                                         cw_ref, cb_ref, wa_ref, ba_ref, wx_ref, bx_ref, lam_ref,
                                         xbuf, h_scr)
        outs.append(out)
    o_ref[...] = _merge_math(x_ref[...], jnp.concatenate(outs, axis=0), att_ref[...], gr_ref[...],
                             ga_ref[...], g1_ref[...], wpr_ref, wpa_ref, wout_ref)

    @pl.when(t == pl.num_programs(1) - 1)
    def _():
        hlast_ref[...] = h_last
        conv_ref[...] = x_last


def _rnn_merge(x, xr, yr, att, gr, ga, mod, layer, rnn_w, w_pr, w_pa, w_out, *, tm):
    batch, seq, d = x.shape
    c = xr.shape[2]
    assert tm % RNN_CHUNK == 0
    tile = lambda a: pl.BlockSpec((None, tm, a.shape[2]), lambda b, t: (b, t, 0))
    weight = lambda w: pl.BlockSpec((None,) + w.shape[1:], lambda b, t: (layer, 0, 0))
    return pl.pallas_call(
        _rnn_merge_kernel,
        grid=(batch, seq // tm),
        in_specs=[tile(x), tile(xr), tile(yr), tile(att), tile(gr), tile(ga),
                  _mod_spec(layer, 2, mod.shape[2], d)] + _rnn_weight_specs(layer, c)
        + [weight(w_pr), weight(w_pa), weight(w_out)],
        out_specs=[
            tile(x),
            pl.BlockSpec((None, 1, c), lambda b, t: (b, 0, 0)),
            pl.BlockSpec((None, CONV_WIDTH - 1, c), lambda b, t: (b, 0, 0)),
        ],
        out_shape=[
            jax.ShapeDtypeStruct(x.shape, F32),
            jax.ShapeDtypeStruct((batch, 1, c), F32),
            jax.ShapeDtypeStruct((batch, CONV_WIDTH - 1, c), F32),
        ],
        scratch_shapes=[pltpu.VMEM((SUBLANES + RNN_CHUNK, c), F32), pltpu.VMEM((1, c), F32)],
        compiler_params=_params(("parallel", "arbitrary")),
    )(x, xr, yr, att, gr, ga, mod, *rnn_w, w_pr, w_pa, w_out)


def _rnn_sample_kernel(xr_ref, yr_ref, hist_ref, h0_ref, cw_ref, cb_ref, wa_ref, ba_ref, wx_ref,
                       bx_ref, lam_ref, out_ref, hnew_ref, conv_ref):
    x = xr_ref[...]
    xc = cb_ref[...] + cw_ref[CONV_WIDTH - 1:CONV_WIDTH, :] * x
    for j in range(CONV_WIDTH - 1):
        xc = xc + cw_ref[j:j + 1, :] * hist_ref[j]
    a, b = _lru_terms(xc, wa_ref, ba_ref, wx_ref, bx_ref, lam_ref)
    h = a * h0_ref[...] + b
    hnew_ref[...] = h
    out_ref[...] = h * _gelu_tanh(yr_ref[...])
    for j in range(CONV_WIDTH - 2):
        conv_ref[j] = hist_ref[j + 1]
    conv_ref[CONV_WIDTH - 2] = x


def _rnn_sample(xr, yr, hist_t, h0, layer, rnn_w):
    n, c = xr.shape
    full = pl.BlockSpec((n, c), lambda i: (0, 0))
    hist_spec = pl.BlockSpec((None, CONV_WIDTH - 1, n, c), lambda i: (layer, 0, 0, 0))
    return pl.pallas_call(
        _rnn_sample_kernel,
        grid=(1,),
        in_specs=[full, full, hist_spec, pl.BlockSpec((None, n, c), lambda i: (layer, 0, 0))]
        + _rnn_weight_specs(layer, c),
        out_specs=[full, full, pl.BlockSpec((CONV_WIDTH - 1, n, c), lambda i: (0, 0, 0))],
        out_shape=[
            jax.ShapeDtypeStruct((n, c), F32),
            jax.ShapeDtypeStruct((n, c), F32),
            jax.ShapeDtypeStruct((CONV_WIDTH - 1, n, c), F32),
        ],
        compiler_params=_params(("arbitrary",)),
    )(xr, yr, hist_t, h0, *rnn_w)


def _attn_prompt_kernel(q_ref, k_ref, v_ref, lq1_ref, lk1_ref, lq2_ref, lk2_ref, g_ref, o_ref,
                        m_scr, acc_scr, s_scr, *, lam_init):
    i = pl.program_id(2)
    tq = q_ref.shape[1]
    tk = tq
    feats = 2 * HEAD_DIM
    heads = q_ref.shape[0] // feats
    feat = lax.broadcasted_iota(jnp.int32, (feats, tq), 0)
    qq = []
    for h in range(heads):
        qt = q_ref[h * feats:(h + 1) * feats, :]
        zero = jnp.zeros_like(qt)
        qq.append(jnp.concatenate([jnp.where(feat < HEAD_DIM, qt, zero),
                                   jnp.where(feat >= HEAD_DIM, qt, zero)], axis=1))
    m_scr[...] = jnp.full(m_scr.shape, MASK_VALUE, F32)
    acc_scr[...] = jnp.zeros_like(acc_scr)
    ones = jnp.ones((acc_scr.shape[1] - V_DIM, tk), BF16)

    def scores(h, j, slot):
        start = pl.multiple_of(j * tk, tk)
        kj = k_ref[pl.ds(start, tk), h * feats:(h + 1) * feats]
        s_scr[h, slot] = jnp.dot(kj, qq[h], preferred_element_type=F32)

    def update(h, j, slot, masked):
        start = pl.multiple_of(j * tk, tk)
        vj = v_ref[h * V_DIM:(h + 1) * V_DIM, pl.ds(start, tk)]
        s = s_scr[h, slot]
        if masked:
            kpos = lax.broadcasted_iota(jnp.int32, s.shape, 0)
            c = lax.broadcasted_iota(jnp.int32, s.shape, 1)
            qpos = jnp.where(c >= tq, c - tq, c)
            s = jnp.where(kpos <= qpos, s, MASK_VALUE)
        m_prev = m_scr[h]
        m_next = jnp.maximum(m_prev, jnp.max(s, axis=0, keepdims=True))
        alpha = jnp.exp2(m_prev - m_next)
        p = jnp.exp2(s - m_next)
        acc_scr[h] = alpha * acc_scr[h] + jnp.dot(jnp.concatenate([vj, ones], axis=0), p.astype(BF16),
                                                  preferred_element_type=F32)
        m_scr[h] = m_next

    def advance(j, slot, masked):
        for h in range(heads):
            if not masked:
                scores(h, j + 1, 1 - slot)
            other = (h + 1) % heads
            update(other, j, slot, masked)

    def group(g, carry):
        for u in range(ATTN_UNROLL):
            advance(g * ATTN_UNROLL + u, u % 2, False)
        return carry

    for h in range(heads):
        scores(h, 0, 0)
    n_groups = i // ATTN_UNROLL
    lax.fori_loop(0, n_groups, group, 0)
    done = n_groups * ATTN_UNROLL
    width = ATTN_UNROLL // 2
    while width >= 2:
        @pl.when((i & width) != 0)
        def _(done=done, width=width):
            for u in range(width):
                advance(done + u, u % 2, False)
        done = done + (i & width)
        width //= 2

    @pl.when((i & 1) != 0)
    def _():
        advance(done, 0, False)
        advance(done + 1, 1, True)

    @pl.when((i & 1) == 0)
    def _():
        advance(done, 0, True)

    lam = _lam_value(lq1_ref, lk1_ref, lq2_ref, lk2_ref, lam_init)
    for h in range(heads):
        o = acc_scr[h, :V_DIM, :] / acc_scr[h, V_DIM:V_DIM + 1, :]
        o = (o[:, :tq] - lam * o[:, tq:]).T
        o_ref[:, h * V_DIM:(h + 1) * V_DIM] = (
            _rms_norm(o, g_ref[...]) * (1.0 - lam_init)).astype(o_ref.dtype)


def _lam_specs(layer):
    idx = lambda *_: (layer, 0, 0)
    return [pl.BlockSpec((None, 1, HEAD_DIM), idx)] * 4 + [pl.BlockSpec((None, 1, V_DIM), idx)]


def _attn_prompt(q, k, v, layer, lam_w, lam_init):
    batch, seq, _ = k.shape
    tq = ATTN_TILE
    hs = ATTN_HEADS
    q_spec = pl.BlockSpec((None, hs * 2 * HEAD_DIM, tq), lambda b, h, i: (b, h, i))
    k_spec = pl.BlockSpec((None, seq, hs * 2 * HEAD_DIM), lambda b, h, i: (b, 0, h))
    v_spec = pl.BlockSpec((None, hs * V_DIM, seq), lambda b, h, i: (b, h, 0))
    return pl.pallas_call(
        functools.partial(_attn_prompt_kernel, lam_init=lam_init),
        grid=(batch, N_DH // hs, seq // tq),
        in_specs=[q_spec, k_spec, v_spec] + _lam_specs(layer),
        out_specs=pl.BlockSpec((None, tq, hs * V_DIM), lambda b, h, i: (b, i, h)),
        out_shape=jax.ShapeDtypeStruct((batch, seq, N_DH * V_DIM), BF16),
        scratch_shapes=[pltpu.VMEM((hs, 1, 2 * tq), F32),
                        pltpu.VMEM((hs, V_DIM + BF16_SUBLANES, 2 * tq), F32),
                        pltpu.VMEM((hs, 2, tq, 2 * tq), F32)],
        compiler_params=_params(("parallel", "parallel", "arbitrary")),
    )(q, k, v, *lam_w)


def _decode_reset(m_scr, l_scr, acc_scr):
    m_scr[...] = jnp.full(m_scr.shape, MASK_VALUE, F32)
    l_scr[...] = jnp.zeros_like(l_scr)
    acc_scr[...] = jnp.zeros_like(acc_scr)


def _decode_query(q_ref):
    width = q_ref.shape[1]
    map_row = lax.broadcasted_iota(jnp.int32, (N_MAPS, width), 0)
    lane = lax.broadcasted_iota(jnp.int32, (N_MAPS, width), 1)
    qbd = jnp.where(lane // HEAD_DIM == map_row, jnp.broadcast_to(q_ref[...], (N_MAPS, width)), 0.0)
    return qbd


def _decode_scores(qbd, k_ref):
    qb = qbd.astype(BF16)
    return jnp.concatenate(
        [jnp.dot(qb, k_ref[i].astype(BF16), preferred_element_type=F32) for i in range(k_ref.shape[0])],
        axis=1)


def _decode_update(s, v_ref, m_scr, l_scr, acc_scr):
    pages = v_ref.shape[0]
    rows = PAGE_SIZE * N_DH
    m_prev = m_scr[...]
    m_next = jnp.maximum(m_prev, jnp.max(s, axis=1, keepdims=True))
    alpha = jnp.exp(m_prev - m_next)
    p = jnp.exp(s - _repeat(m_next, pages, axis=1))
    l_scr[...] = alpha * l_scr[...] + jnp.sum(p, axis=1, keepdims=True)
    p_rows = jnp.concatenate([p[:, i * PAGE_SIZE:(i + 1) * PAGE_SIZE] for i in range(pages)],
                             axis=0).astype(BF16)
    token = lax.broadcasted_iota(jnp.int32, (PAGE_SIZE, rows), 0)
    row = lax.broadcasted_iota(jnp.int32, (PAGE_SIZE, rows), 1)
    expand = jnp.where(row // N_DH == token, 1.0, 0.0).astype(BF16)
    spread = jnp.dot(p_rows, expand, preferred_element_type=F32)
    r = lax.broadcasted_iota(jnp.int32, spread.shape, 0)
    c = lax.broadcasted_iota(jnp.int32, spread.shape, 1)
    spread = jnp.where(c % N_DH == (r % N_MAPS) // 2, spread, 0.0)
    weights = jnp.concatenate([spread[i * N_MAPS:(i + 1) * N_MAPS, :] for i in range(pages)],
                              axis=1).astype(BF16)
    values = v_ref[...].reshape(pages * rows, V_DIM).astype(BF16)
    acc_scr[...] = alpha * acc_scr[...] + jnp.dot(weights, values, preferred_element_type=F32)
    m_scr[...] = m_next


def _decode_finish(qbd, kn_ref, vn_ref, lam_refs, g_ref, m_scr, l_scr, acc_scr, lam_init):
    s_new = jnp.sum(qbd * kn_ref[...], axis=1, keepdims=True)
    m_prev = m_scr[...]
    m_fin = jnp.maximum(m_prev, s_new)
    alpha = jnp.exp(m_prev - m_fin)
    p_new = jnp.exp(s_new - m_fin)
    l_fin = alpha * l_scr[...] + p_new
    vn = vn_ref[...]
    vn_rows = jnp.concatenate([vn[:, (j // 2) * V_DIM:(j // 2 + 1) * V_DIM] for j in range(N_MAPS)],
                              axis=0)
    res = (alpha * acc_scr[...] + p_new * vn_rows) / l_fin
    lam = _lam_value(*lam_refs, lam_init)
    g = g_ref[...]
    return jnp.concatenate(
        [_rms_norm(res[2 * h:2 * h + 1, :] - lam * res[2 * h + 1:2 * h + 2, :], g) for h in range(N_DH)],
        axis=1) * (1.0 - lam_init)


def _merge_math(x, rnn_out, att, gr, ga, g1, wpr_ref, wpa_ref, wout_ref):
    pr = jnp.dot(rnn_out.astype(BF16), wpr_ref[...], preferred_element_type=F32)
    pa = jnp.dot(att.astype(BF16), wpa_ref[...], preferred_element_type=F32)
    merged = _sigmoid(gr.astype(F32)) * pr + _sigmoid(ga.astype(F32)) * pa
    out = jnp.dot(merged.astype(BF16), wout_ref[...], preferred_element_type=F32)
    return x + g1 * out


def _merge_kernel(x_ref, rnn_ref, att_ref, gr_ref, ga_ref, g1_ref, wpr_ref, wpa_ref, wout_ref, o_ref):
    o_ref[...] = _merge_math(x_ref[...], rnn_ref[...], att_ref[...], gr_ref[...], ga_ref[...],
                             g1_ref[...], wpr_ref, wpa_ref, wout_ref)


def _merge(x, rnn_out, att, gr, ga, mod, layer, w_pr, w_pa, w_out, *, tm):
    groups, rows, d = x.shape
    tile = lambda a: pl.BlockSpec((None, tm, a.shape[2]), lambda g, i: (g, i, 0))
    weight = lambda w: pl.BlockSpec((None,) + w.shape[1:], lambda g, i: (layer, 0, 0))
    return pl.pallas_call(
        _merge_kernel,
        grid=(groups, rows // tm),
        in_specs=[tile(x), tile(rnn_out), tile(att), tile(gr), tile(ga),
                  _mod_spec(layer, 2, mod.shape[2], d), weight(w_pr), weight(w_pa), weight(w_out)],
        out_specs=tile(x),
        out_shape=jax.ShapeDtypeStruct(x.shape, F32),
        compiler_params=_params(("parallel", "parallel")),
    )(x, rnn_out, att, gr, ga, mod, w_pr, w_pa, w_out)


def _mlp_hidden(x, sh_ref, sc_ref, ng_ref):
    return (_rms_norm(x, ng_ref[...]) * (1.0 + sc_ref[...]) + sh_ref[...]).astype(BF16)


def _ff_up(hb, wup_ref, lo):
    u = jnp.maximum(jnp.dot(hb, wup_ref[:, lo:lo + FF_CHUNK], preferred_element_type=F32), 0.0)
    return (u * u).astype(BF16)


def _ff_down(u, wdn_ref, lo):
    return jnp.dot(u, wdn_ref[lo:lo + FF_CHUNK, :], preferred_element_type=F32)


def _ff_chunk(hb, wup_ref, wdn_ref, lo):
    return _ff_down(_ff_up(hb, wup_ref, lo), wdn_ref, lo)


def _mlp_finish(x, acc, g2_ref, fg_ref, o_ref):
    y = x + g2_ref[...] * acc
    o_ref[...] = y if fg_ref is None else _rms_norm(y, fg_ref[...])


def _mlp_kernel(x_ref, sh_ref, sc_ref, g2_ref, ng_ref, wup_ref, wdn_ref, *rest, final):
    fg_ref = rest[0] if final else None
    o_ref = rest[-1]
    x = x_ref[...]
    hb = _mlp_hidden(x, sh_ref, sc_ref, ng_ref)
    acc = jnp.zeros(x.shape, F32)
    for lo in range(0, wup_ref.shape[1], FF_CHUNK):
        acc = acc + _ff_chunk(hb, wup_ref, wdn_ref, lo)
    _mlp_finish(x, acc, g2_ref, fg_ref, o_ref)


def _mlp_decode_kernel(pt_ref, x_ref, sh_ref, sc_ref, g2_ref, ng_ref, wup_ref, wdn_ref, *rest,
                       final, lam_init, layer, n_pages):
    fg_ref = rest[0] if final else None
    (q_ref, kn_ref, vn_ref, lq1_ref, lk1_ref, lq2_ref, lk2_ref, sg_ref, ck_ref, cv_ref,
     o_ref, os_ref, kbuf, vbuf, sem, m_scr, l_scr, acc_scr) = rest[1 if final else 0:]
    steps = pl.num_programs(1)
    t = pl.program_id(0) * steps + pl.program_id(1)
    total = pl.num_programs(0) * steps
    n_chunks = n_pages // DECODE_PAGES
    n_ff = wup_ref.shape[1] // FF_CHUNK

    def copies(which, seq, chunk, slot):
        src, dst = ((ck_ref, kbuf), (cv_ref, vbuf))[which]
        return [pltpu.make_async_copy(src.at[layer, pt_ref[seq * n_pages + chunk * DECODE_PAGES + p]],
                                      dst.at[slot, p], sem.at[slot, which])
                for p in range(DECODE_PAGES)]

    def start(which, seq, chunk, slot):
        for copy in copies(which, seq, chunk, slot):
            copy.start()

    def wait(which, seq, chunk, slot):
        for copy in copies(which, seq, chunk, slot):
            copy.wait()

    def start_ahead(which, c, slot):
        if c + DECODE_SLOTS < n_chunks:
            start(which, t, c + DECODE_SLOTS, slot)
        else:
            @pl.when(t + 1 < total)
            def _():
                start(which, t + 1, c + DECODE_SLOTS - n_chunks, slot)

    @pl.when(t == 0)
    def _():
        for c in range(DECODE_SLOTS):
            start(0, 0, c, c)
            start(1, 0, c, c)

    qbd = _decode_query(q_ref)
    _decode_reset(m_scr, l_scr, acc_scr)
    x = x_ref[...]
    hb = _mlp_hidden(x, sh_ref, sc_ref, ng_ref)
    acc = jnp.zeros(x.shape, F32)
    u = None
    n_pieces = 2 * n_ff
    for c in range(n_chunks):
        slot = c % DECODE_SLOTS
        wait(0, t, c, slot)
        s = _decode_scores(qbd, kbuf.at[slot])
        start_ahead(0, c, slot)
        for piece in range(c * n_pieces // n_chunks, (c + 1) * n_pieces // n_chunks):
            lo = (piece // 2) * FF_CHUNK
            if piece % 2 == 0:
                u = _ff_up(hb, wup_ref, lo)
            else:
                acc = acc + _ff_down(u, wdn_ref, lo)
        wait(1, t, c, slot)
        _decode_update(s, vbuf.at[slot], m_scr, l_scr, acc_scr)
        start_ahead(1, c, slot)
    os_ref[...] = _decode_finish(qbd, kn_ref, vn_ref,
                                 (lq1_ref, lk1_ref, lq2_ref, lk2_ref), sg_ref,
                                 m_scr, l_scr, acc_scr, lam_init)
    _mlp_finish(x, acc, g2_ref, fg_ref, o_ref)


def _mlp(x, mod, layer, norm_g, w_up, w_down, final_g, *, tm, final):
    groups, rows, d = x.shape
    tile = pl.BlockSpec((None, tm, d), lambda g, i: (g, i, 0))
    weight = lambda w: pl.BlockSpec((None,) + w.shape[1:], lambda g, i: (layer, 0, 0))
    mod_rows = mod.shape[2]
    in_specs = [tile, _mod_spec(layer, 3, mod_rows, d), _mod_spec(layer, 4, mod_rows, d),
                _mod_spec(layer, 5, mod_rows, d), pl.BlockSpec((None, 1, d), lambda g, i: (layer, 0, 0)),
                weight(w_up), weight(w_down)]
    args = [x, mod, mod, mod, norm_g, w_up, w_down]
    if final:
        in_specs.append(pl.BlockSpec((1, d), lambda g, i: (0, 0)))
        args.append(final_g)
    return pl.pallas_call(
        functools.partial(_mlp_kernel, final=final),
        grid=(groups, rows // tm),
        in_specs=in_specs,
        out_specs=tile,
        out_shape=jax.ShapeDtypeStruct(x.shape, F32),
        compiler_params=_params(("parallel", "parallel")),
    )(*args)


def _mlp_decode(x, mod, layer, norm_g, w_up, w_down, final_g, q, k_new, v_new, cache_k, cache_v,
                page_table, lam_w, lam_init, *, tm, final):
    groups, rows, d = x.shape
    n, _, width = q.shape
    n_pages = page_table.shape[1]
    steps = rows // tm
    assert n == groups * steps, "one running sequence per MLP grid step"
    assert n_pages % (DECODE_SLOTS * DECODE_PAGES) == 0, "page chunks rotate through the slots"
    mod_rows = mod.shape[2]
    tile = pl.BlockSpec((None, tm, d), lambda g, i, pt: (g, i, 0))
    mod_spec = lambda which: pl.BlockSpec((None, None, mod_rows, d), lambda g, i, pt: (layer, g, 0, which))
    per_layer = lambda shape, **kw: pl.BlockSpec((None,) + shape, lambda g, i, pt: (layer,) + (0,) * len(shape), **kw)
    row = pl.BlockSpec((None, 1, width), lambda g, i, pt: (g * steps + i, 0, 0))
    hbm = pl.BlockSpec(memory_space=pl.ANY)
    in_specs = [tile, mod_spec(3), mod_spec(4), mod_spec(5), per_layer((1, d)),
                per_layer(w_up.shape[1:], pipeline_mode=pl.Buffered(1)),
                per_layer(w_down.shape[1:], pipeline_mode=pl.Buffered(1))]
    args = [x, mod, mod, mod, norm_g, w_up, w_down]
    if final:
        in_specs.append(pl.BlockSpec((1, d), lambda g, i, pt: (0, 0)))
        args.append(final_g)
    in_specs += [row, row, row] + [per_layer((1, HEAD_DIM))] * 4 + [per_layer((1, V_DIM)), hbm, hbm]
    args += [q, k_new, v_new, *lam_w, cache_k, cache_v]
    grid_spec = pltpu.PrefetchScalarGridSpec(
        num_scalar_prefetch=1,
        grid=(groups, steps),
        in_specs=in_specs,
        out_specs=[tile, row],
        scratch_shapes=[pltpu.VMEM((DECODE_SLOTS, DECODE_PAGES, width, PAGE_SIZE), F32),
                        pltpu.VMEM((DECODE_SLOTS, DECODE_PAGES, PAGE_SIZE * N_DH, V_DIM), F32),
                        pltpu.SemaphoreType.DMA((DECODE_SLOTS, 2)),
                        pltpu.VMEM((N_MAPS, LANES), F32), pltpu.VMEM((N_MAPS, LANES), F32),
                        pltpu.VMEM((N_MAPS, V_DIM), F32)],
    )
    return pl.pallas_call(
        functools.partial(_mlp_decode_kernel, final=final, lam_init=lam_init, layer=layer,
                          n_pages=n_pages),
        grid_spec=grid_spec,
        out_shape=[jax.ShapeDtypeStruct(x.shape, F32), jax.ShapeDtypeStruct((n, 1, width), F32)],
        compiler_params=_params(("arbitrary", "arbitrary")),
    )(page_table.reshape(-1), *args)


def kernel(x_prompt, x_sample, c_prompt, c_sample, cache_k, cache_v, state_rnn, state_conv, page_table, w_ada, b_ada, norm1_g, norm2_g, w_in, conv_w, conv_b, w_a, b_a, w_x, b_x, lru_lambda, lq1, lk1, lq2, lk2, subln_g, w_pr, w_pa, w_out, w_up, w_down, final_g):
    depth, d = norm1_g.shape
    batch, seq, _ = x_prompt.shape
    n_dec = x_sample.shape[0]
    d_att = N_DH * 2 * HEAD_DIM
    assert x_sample.shape[1] == 1, "one new token per running sequence"
    assert seq % ROW_TILE == 0 and seq % RNN_CHUNK == 0 and seq % ATTN_TILE == 0
    assert page_table.shape[1] % DECODE_PAGES == 0 and cache_k.shape[2] == PAGE_SIZE

    w_in_b, w_pr_b, w_pa_b, w_out_b, w_up_b, w_down_b, w_a_b, w_x_b = (
        w.astype(BF16) for w in (w_in, w_pr, w_pa, w_out, w_up, w_down, w_a, w_x))
    row3 = lambda a: a.reshape(depth, 1, a.shape[-1])
    rnn_w = (conv_w, row3(conv_b), w_a_b, row3(b_a), w_x_b, row3(b_x), row3(lru_lambda))
    lam_w = (row3(lq1), row3(lk1), row3(lq2), row3(lk2), row3(subln_g))
    norm1, norm2 = row3(norm1_g), row3(norm2_g)
    final_g2 = final_g.reshape(1, d)

    n_cond = n_dec + batch
    pad = -n_cond % (2 * SUBLANES)
    c_all = jnp.concatenate([c_sample, c_prompt, jnp.zeros((pad, d), F32)], axis=0)
    mods = _ada_mod(c_all, w_ada, b_ada)
    mod_s = mods[:, :n_dec].reshape(depth, 1, n_dec, 6 * d)
    mod_p = mods[:, n_dec:n_cond].reshape(depth, batch, 1, 6 * d)

    ck = jnp.transpose(cache_k, (0, 1, 3, 4, 5, 2)).reshape(depth, cache_k.shape[1], d_att, PAGE_SIZE)
    cv = cache_v.reshape(depth, cache_v.shape[1], PAGE_SIZE * N_DH, V_DIM)
    hist_t = jnp.swapaxes(state_conv, 1, 2)

    xp = x_prompt
    xs = x_sample.reshape(1, n_dec, d)
    h_p, cv_p, k_s, v_s, h_s, cv_s = ([] for _ in range(6))
    kv_p = None
    for l in range(depth):
        lam_init = 0.8 - 0.6 * math.exp(-0.3 * l)
        last = l == depth - 1

        xr, yr, qt, k_all, v_all, gr, ga, kb, vtb = _in_proj(
            xp, mod_p, l, norm1, w_in_b, tm=ROW_TILE, prompt=True, stacked=kv_p)
        kv_p = (k_all, v_all)
        att = _attn_prompt(qt, kb, vtb, l, lam_w, lam_init)
        xp, h_last, conv_last = _rnn_merge(xp, xr, yr, att, gr, ga, mod_p, l, rnn_w,
                                           w_pr_b, w_pa_b, w_out_b, tm=ROW_TILE)
        h_p.append(h_last[:, 0]); cv_p.append(conv_last)

        xr, yr, q, k, v, gr, ga = _in_proj(xs, mod_s, l, norm1, w_in_b, tm=n_dec, prompt=False)
        rnn_out, h_new, conv_new = _rnn_sample(xr[0], yr[0], hist_t, state_rnn, l, rnn_w)
        as_rows = lambda a: a.reshape(n_dec, 1, a.shape[-1])
        xp, att = _mlp_decode(xp, mod_p, l, norm2, w_up_b, w_down_b, final_g2, as_rows(q), as_rows(k),
                              as_rows(v), ck, cv, page_table, lam_w, lam_init, tm=ROW_TILE, final=last)
        xs = _merge(xs, rnn_out[None], att.reshape(1, n_dec, -1), gr, ga, mod_s, l,
                    w_pr_b, w_pa_b, w_out_b, tm=n_dec)
        xs = _mlp(xs, mod_s, l, norm2, w_up_b, w_down_b, final_g2, tm=n_dec, final=last)
        k_s.append(k[0]); v_s.append(v[0]); h_s.append(h_new); cv_s.append(jnp.swapaxes(conv_new, 0, 1))

    kshape = lambda n, t: (depth, n, t, N_DH, 2, HEAD_DIM)
    vshape = lambda n, t: (depth, n, t, N_DH, V_DIM)
    k_prompt = jnp.transpose(kv_p[0].reshape(depth, batch, N_DH, 2, HEAD_DIM, seq), (0, 1, 5, 2, 3, 4))
    return (xp, xs.reshape(n_dec, 1, d),
            k_prompt, kv_p[1].reshape(vshape(batch, seq)),
            jnp.stack(h_p), jnp.stack(cv_p),
            jnp.stack(k_s).reshape(kshape(n_dec, 1)), jnp.stack(v_s).reshape(vshape(n_dec, 1)),
            jnp.stack(h_s), jnp.stack(cv_s))
```

```python
import functools
import math

import jax
import jax.numpy as jnp
from jax import lax
from jax.experimental import pallas as pl
from jax.experimental.pallas import tpu as pltpu

F32 = jnp.float32
BF16 = jnp.bfloat16

EPS = 1e-6
LRU_C = 8.0
N_RNN_BLOCKS = 8
CONV_WIDTH = 4
N_DH = 4
HEAD_DIM = 64
V_DIM = 2 * HEAD_DIM
N_MAPS = 2 * N_DH
PAGE_SIZE = 128
MASK_VALUE = -1e30
LOG2_E = math.log2(math.e)

V7X_VMEM_BYTES = 64 * 1024 * 1024
VMEM_LIMIT_BYTES = V7X_VMEM_BYTES - 8 * 1024 * 1024
SUBLANES = 8
BF16_SUBLANES = 2 * SUBLANES
LANES = 128

ROW_TILE = 512
RNN_CHUNK = 256
ATTN_TILE = 256
ATTN_UNROLL = 4
ATTN_HEADS = 4
DECODE_PAGES = 16
DECODE_SLOTS = 2
ADA_TILE = 1536
FF_CHUNK = 1024


def _params(semantics):
    return pltpu.CompilerParams(dimension_semantics=semantics, vmem_limit_bytes=VMEM_LIMIT_BYTES)


def _repeat(x, n, axis):
    return jnp.concatenate([x] * n, axis=axis)


def _rms_norm(x, g):
    return x * lax.rsqrt(jnp.mean(x * x, axis=-1, keepdims=True) + EPS) * g


def _lam_value(lq1_ref, lk1_ref, lq2_ref, lk2_ref, lam_init):
    s1 = jnp.sum(lq1_ref[...] * lk1_ref[...], axis=1, keepdims=True)
    s2 = jnp.sum(lq2_ref[...] * lk2_ref[...], axis=1, keepdims=True)
    return jnp.exp(s1) - jnp.exp(s2) + lam_init


def _ada_kernel(c_ref, w_ref, b_ref, o_ref):
    c = c_ref[...]
    a = (c * jax.nn.sigmoid(c)).astype(BF16)
    o_ref[...] = jnp.dot(a, w_ref[...].astype(BF16), preferred_element_type=F32) + b_ref[...]


def _ada_mod(c_all, w_ada, b_ada):
    depth, d, n = w_ada.shape
    rows = c_all.shape[0]
    return pl.pallas_call(
        _ada_kernel,
        grid=(depth, n // ADA_TILE),
        in_specs=[
            pl.BlockSpec((rows, d), lambda l, j: (0, 0)),
            pl.BlockSpec((None, d, ADA_TILE), lambda l, j: (l, 0, j)),
            pl.BlockSpec((None, 1, ADA_TILE), lambda l, j: (l, 0, j)),
        ],
        out_specs=pl.BlockSpec((None, rows, ADA_TILE), lambda l, j: (l, 0, j)),
        out_shape=jax.ShapeDtypeStruct((depth, rows, n), F32),
        compiler_params=_params(("arbitrary", "arbitrary")),
    )(c_all, w_ada, b_ada.reshape(depth, 1, n))


def _mod_spec(layer, which, rows, d):
    return pl.BlockSpec((None, None, rows, d), lambda g, i: (layer, g, 0, which))


def _stream_columns(d):
    widths = [d, d, N_DH * 2 * HEAD_DIM, N_DH * 2 * HEAD_DIM, N_DH * V_DIM, d, d]
    los = [sum(widths[:i]) for i in range(len(widths))]
    return [(lo, lo + w) for lo, w in zip(los, widths)]


def _in_proj_kernel(x_ref, sh_ref, sc_ref, g_ref, w_ref, *refs, prompt):
    hn = _rms_norm(x_ref[...], g_ref[...]) * (1.0 + sc_ref[...]) + sh_ref[...]
    hb = hn.astype(BF16)
    cols = _stream_columns(x_ref.shape[1])
    stream = lambda idx: jnp.dot(hb, w_ref[:, cols[idx][0]:cols[idx][1]], preferred_element_type=F32)
    q_scale = HEAD_DIM ** -0.5 * (LOG2_E if prompt else 1.0)
    q = stream(2) * q_scale
    k = stream(3)
    v = stream(4)
    if prompt:
        xr_ref, yr_ref, q_ref, k_ref, v_ref, gr_ref, ga_ref, kb_ref, vb_ref = refs[-9:]
        q_ref[...] = q.T.astype(BF16)
        k_ref[...] = k.T
        kb_ref[...] = k.astype(BF16)
        for h in range(N_DH):
            v_ref[pl.ds(h, v.shape[0], stride=N_DH), :] = v[:, h * V_DIM:(h + 1) * V_DIM]
        vb_ref[...] = v.T.astype(BF16)
    else:
        xr_ref, yr_ref, q_ref, k_ref, v_ref, gr_ref, ga_ref = refs
        q_ref[...] = q
        k_ref[...] = k
        v_ref[...] = v
    xr_ref[...] = stream(0)
    yr_ref[...] = stream(1).astype(yr_ref.dtype)
    gr_ref[...] = stream(5).astype(gr_ref.dtype)
    ga_ref[...] = stream(6).astype(ga_ref.dtype)


def _in_proj(x, mod, layer, norm_g, w_in, *, tm, prompt, stacked=None):
    groups, rows, d = x.shape
    mod_rows = mod.shape[2]
    widths = [hi - lo for lo, hi in _stream_columns(d)]
    inter = BF16 if prompt else F32
    dtypes = [F32, inter, inter, F32, F32, inter, inter]
    tile = lambda w: pl.BlockSpec((None, tm, w), lambda g, i: (g, i, 0))
    tile_t = lambda w: pl.BlockSpec((None, w, tm), lambda g, i: (g, 0, i))
    out_shapes = [jax.ShapeDtypeStruct((groups, rows, w), t) for w, t in zip(widths, dtypes)]
    out_specs = [tile(w) for w in widths]
    in_specs = [
        tile(d),
        _mod_spec(layer, 0, mod_rows, d),
        _mod_spec(layer, 1, mod_rows, d),
        pl.BlockSpec((None, 1, d), lambda g, i: (layer, 0, 0)),
        pl.BlockSpec((None, d, w_in.shape[2]), lambda g, i: (layer, 0, 0)),
    ]
    args = [x, mod, mod, norm_g, w_in]
    aliases = {}
    if prompt:
        depth = w_in.shape[0]
        out_shapes[2] = jax.ShapeDtypeStruct((groups, widths[2], rows), BF16)
        out_specs[2] = tile_t(widths[2])
        out_shapes[3] = jax.ShapeDtypeStruct((depth, groups, widths[3], rows), F32)
        out_specs[3] = pl.BlockSpec((None, None, widths[3], tm), lambda g, i: (layer, g, 0, i))
        out_shapes[4] = jax.ShapeDtypeStruct((depth, groups, rows * N_DH, V_DIM), F32)
        out_specs[4] = pl.BlockSpec((None, None, tm * N_DH, V_DIM), lambda g, i: (layer, g, i, 0))
        out_shapes += [jax.ShapeDtypeStruct((groups, rows, widths[3]), BF16),
                       jax.ShapeDtypeStruct((groups, widths[4], rows), BF16)]
        out_specs += [tile(widths[3]), tile_t(widths[4])]
        if stacked is not None:
            aliases = {len(args): 3, len(args) + 1: 4}
            in_specs += [pl.BlockSpec(memory_space=pl.ANY)] * 2
            args += list(stacked)
    return pl.pallas_call(
        functools.partial(_in_proj_kernel, prompt=prompt),
        grid=(groups, rows // tm),
        in_specs=in_specs,
        out_specs=out_specs,
        out_shape=out_shapes,
        input_output_aliases=aliases,
        compiler_params=_params(("parallel", "parallel")),
    )(*args)


def _sigmoid(z):
    return 0.5 * jnp.tanh(0.5 * z) + 0.5


def _softplus(z):
    return jnp.maximum(z, 0.0) + jnp.log1p(jnp.exp(-jnp.abs(z)))


def _block_diag_dot(xb, w_ref):
    blk = xb.shape[1] // N_RNN_BLOCKS
    return jnp.concatenate(
        [jnp.dot(xb[:, n * blk:(n + 1) * blk], w_ref[n], preferred_element_type=F32)
         for n in range(N_RNN_BLOCKS)], axis=1)


def _lru_terms(xc, wa_ref, ba_ref, wx_ref, bx_ref, lam_ref):
    xb = xc.astype(BF16)
    r = _sigmoid(_block_diag_dot(xb, wa_ref) + ba_ref[...])
    i = _sigmoid(_block_diag_dot(xb, wx_ref) + bx_ref[...])
    log_a = -LRU_C * r * _softplus(-lam_ref[...])
    a = jnp.exp(log_a)
    u = -jnp.tanh(log_a) * (a * a + 1.0)
    b = jnp.where(u > 0.0, u * lax.rsqrt(u), 0.0) * (i * xc)
    return a, b


def _gelu_tanh(y):
    return 0.5 * y * (1.0 + jnp.tanh(math.sqrt(2.0 / math.pi) * (y + 0.044715 * (y * y * y))))


def _rnn_reset(xbuf, h_scr):
    xbuf[0:SUBLANES, :] = jnp.zeros((SUBLANES, xbuf.shape[1]), F32)
    h_scr[...] = jnp.zeros_like(h_scr)


def _rnn_chunk(x, y, cw_ref, cb_ref, wa_ref, ba_ref, wx_ref, bx_ref, lam_ref, xbuf, h_scr):
    steps = x.shape[0]
    hist = SUBLANES
    xbuf[hist:hist + steps, :] = x
    xc = cb_ref[...] + cw_ref[CONV_WIDTH - 1:CONV_WIDTH, :] * x
    for j in range(CONV_WIDTH - 1):
        back = CONV_WIDTH - 1 - j
        xc = xc + cw_ref[j:j + 1, :] * xbuf[hist - back:hist - back + steps, :]
    xbuf[0:hist, :] = x[steps - hist:steps, :]

    a, b = _lru_terms(xc, wa_ref, ba_ref, wx_ref, bx_ref, lam_ref)
    groups = steps // SUBLANES
    a = a.reshape(groups, SUBLANES, a.shape[1])
    b = b.reshape(groups, SUBLANES, b.shape[1])
    sub = lax.broadcasted_iota(jnp.int32, (1, SUBLANES, 1), 1)
    shift = 1
    while shift < SUBLANES:
        a_prev = pltpu.roll(a, shift, axis=1)
        b_prev = pltpu.roll(b, shift, axis=1)
        valid = sub >= shift
        b = jnp.where(valid, a * b_prev + b, b)
        a = jnp.where(valid, a * a_prev, a)
        shift *= 2
    h_prev = h_scr[...]
    hs = []
    for g in range(groups):
        h_g = a[g] * h_prev + b[g]
        hs.append(h_g)
        h_prev = h_g[SUBLANES - 1:SUBLANES, :]
    h = jnp.concatenate(hs, axis=0)
    h_scr[...] = h_prev
    return (h * _gelu_tanh(y.astype(F32))).astype(BF16), h_prev, x[steps - (CONV_WIDTH - 1):steps, :]


def _rnn_weight_specs(layer, c):
    idx2 = lambda *_: (layer, 0, 0)
    idx3 = lambda *_: (layer, 0, 0, 0)
    blk = c // N_RNN_BLOCKS
    return [
        pl.BlockSpec((None, CONV_WIDTH, c), idx2),
        pl.BlockSpec((None, 1, c), idx2),
        pl.BlockSpec((None, N_RNN_BLOCKS, blk, blk), idx3),
        pl.BlockSpec((None, 1, c), idx2),
        pl.BlockSpec((None, N_RNN_BLOCKS, blk, blk), idx3),
        pl.BlockSpec((None, 1, c), idx2),
        pl.BlockSpec((None, 1, c), idx2),
    ]


def _rnn_merge_kernel(x_ref, xr_ref, yr_ref, att_ref, gr_ref, ga_ref, g1_ref,
                      cw_ref, cb_ref, wa_ref, ba_ref, wx_ref, bx_ref, lam_ref, wpr_ref, wpa_ref, wout_ref,
                      o_ref, hlast_ref, conv_ref, xbuf, h_scr):
    t = pl.program_id(1)

    @pl.when(t == 0)
    def _():
        _rnn_reset(xbuf, h_scr)

    outs = []
    for r in range(0, xr_ref.shape[0], RNN_CHUNK):
        out, h_last, x_last = _rnn_chunk(xr_ref[r:r + RNN_CHUNK, :], yr_ref[r:r + RNN_CHUNK, :],
                                         cw_ref, cb_ref, wa_ref, ba_ref, wx_ref, bx_ref, lam_ref,
                                         xbuf, h_scr)
        outs.append(out)
    o_ref[...] = _merge_math(x_ref[...], jnp.concatenate(outs, axis=0), att_ref[...], gr_ref[...],
                             ga_ref[...], g1_ref[...], wpr_ref, wpa_ref, wout_ref)

    @pl.when(t == pl.num_programs(1) - 1)
    def _():
        hlast_ref[...] = h_last
        conv_ref[...] = x_last


def _rnn_merge(x, xr, yr, att, gr, ga, mod, layer, rnn_w, w_pr, w_pa, w_out, *, tm):
    batch, seq, d = x.shape
    c = xr.shape[2]
    assert tm % RNN_CHUNK == 0
    tile = lambda a: pl.BlockSpec((None, tm, a.shape[2]), lambda b, t: (b, t, 0))
    weight = lambda w: pl.BlockSpec((None,) + w.shape[1:], lambda b, t: (layer, 0, 0))
    return pl.pallas_call(
        _rnn_merge_kernel,
        grid=(batch, seq // tm),
        in_specs=[tile(x), tile(xr), tile(yr), tile(att), tile(gr), tile(ga),
                  _mod_spec(layer, 2, mod.shape[2], d)] + _rnn_weight_specs(layer, c)
        + [weight(w_pr), weight(w_pa), weight(w_out)],
        out_specs=[
            tile(x),
            pl.BlockSpec((None, 1, c), lambda b, t: (b, 0, 0)),
            pl.BlockSpec((None, CONV_WIDTH - 1, c), lambda b, t: (b, 0, 0)),
        ],
        out_shape=[
            jax.ShapeDtypeStruct(x.shape, F32),
            jax.ShapeDtypeStruct((batch, 1, c), F32),
            jax.ShapeDtypeStruct((batch, CONV_WIDTH - 1, c), F32),
        ],
        scratch_shapes=[pltpu.VMEM((SUBLANES + RNN_CHUNK, c), F32), pltpu.VMEM((1, c), F32)],
        compiler_params=_params(("parallel", "arbitrary")),
    )(x, xr, yr, att, gr, ga, mod, *rnn_w, w_pr, w_pa, w_out)


def _rnn_sample_kernel(xr_ref, yr_ref, hist_ref, h0_ref, cw_ref, cb_ref, wa_ref, ba_ref, wx_ref,
                       bx_ref, lam_ref, out_ref, hnew_ref, conv_ref):
    x = xr_ref[...]
    xc = cb_ref[...] + cw_ref[CONV_WIDTH - 1:CONV_WIDTH, :] * x
    for j in range(CONV_WIDTH - 1):
        xc = xc + cw_ref[j:j + 1, :] * hist_ref[j]
    a, b = _lru_terms(xc, wa_ref, ba_ref, wx_ref, bx_ref, lam_ref)
    h = a * h0_ref[...] + b
    hnew_ref[...] = h
    out_ref[...] = h * _gelu_tanh(yr_ref[...])
    for j in range(CONV_WIDTH - 2):
        conv_ref[j] = hist_ref[j + 1]
    conv_ref[CONV_WIDTH - 2] = x


def _rnn_sample(xr, yr, hist_t, h0, layer, rnn_w):
    n, c = xr.shape
    full = pl.BlockSpec((n, c), lambda i: (0, 0))
    hist_spec = pl.BlockSpec((None, CONV_WIDTH - 1, n, c), lambda i: (layer, 0, 0, 0))
    return pl.pallas_call(
        _rnn_sample_kernel,
        grid=(1,),
        in_specs=[full, full, hist_spec, pl.BlockSpec((None, n, c), lambda i: (layer, 0, 0))]
        + _rnn_weight_specs(layer, c),
        out_specs=[full, full, pl.BlockSpec((CONV_WIDTH - 1, n, c), lambda i: (0, 0, 0))],
        out_shape=[
            jax.ShapeDtypeStruct((n, c), F32),
            jax.ShapeDtypeStruct((n, c), F32),
            jax.ShapeDtypeStruct((CONV_WIDTH - 1, n, c), F32),
        ],
        compiler_params=_params(("arbitrary",)),
    )(xr, yr, hist_t, h0, *rnn_w)


def _attn_prompt_kernel(q_ref, k_ref, v_ref, lq1_ref, lk1_ref, lq2_ref, lk2_ref, g_ref, o_ref,
                        m_scr, acc_scr, s_scr, *, lam_init):
    i = pl.program_id(2)
    tq = q_ref.shape[1]
    tk = tq
    feats = 2 * HEAD_DIM
    heads = q_ref.shape[0] // feats
    feat = lax.broadcasted_iota(jnp.int32, (feats, tq), 0)
    qq = []
    for h in range(heads):
        qt = q_ref[h * feats:(h + 1) * feats, :]
        zero = jnp.zeros_like(qt)
        qq.append(jnp.concatenate([jnp.where(feat < HEAD_DIM, qt, zero),
                                   jnp.where(feat >= HEAD_DIM, qt, zero)], axis=1))
    m_scr[...] = jnp.full(m_scr.shape, MASK_VALUE, F32)
    acc_scr[...] = jnp.zeros_like(acc_scr)
    ones = jnp.ones((acc_scr.shape[1] - V_DIM, tk), BF16)

    def scores(h, j, slot):
        start = pl.multiple_of(j * tk, tk)
        kj = k_ref[pl.ds(start, tk), h * feats:(h + 1) * feats]
        s_scr[h, slot] = jnp.dot(kj, qq[h], preferred_element_type=F32)

    def update(h, j, slot, masked):
        start = pl.multiple_of(j * tk, tk)
        vj = v_ref[h * V_DIM:(h + 1) * V_DIM, pl.ds(start, tk)]
        s = s_scr[h, slot]
        if masked:
            kpos = lax.broadcasted_iota(jnp.int32, s.shape, 0)
            c = lax.broadcasted_iota(jnp.int32, s.shape, 1)
            qpos = jnp.where(c >= tq, c - tq, c)
            s = jnp.where(kpos <= qpos, s, MASK_VALUE)
        m_prev = m_scr[h]
        m_next = jnp.maximum(m_prev, jnp.max(s, axis=0, keepdims=True))
        alpha = jnp.exp2(m_prev - m_next)
        p = jnp.exp2(s - m_next)
        acc_scr[h] = alpha * acc_scr[h] + jnp.dot(jnp.concatenate([vj, ones], axis=0), p.astype(BF16),
                                                  preferred_element_type=F32)
        m_scr[h] = m_next

    def advance(j, slot, masked):
        for h in range(heads):
            if not masked:
                scores(h, j + 1, 1 - slot)
            other = (h + 1) % heads
            update(other, j, slot, masked)

    def group(g, carry):
        for u in range(ATTN_UNROLL):
            advance(g * ATTN_UNROLL + u, u % 2, False)
        return carry

    for h in range(heads):
        scores(h, 0, 0)
    n_groups = i // ATTN_UNROLL
    lax.fori_loop(0, n_groups, group, 0)
    done = n_groups * ATTN_UNROLL
    width = ATTN_UNROLL // 2
    while width >= 2:
        @pl.when((i & width) != 0)
        def _(done=done, width=width):
            for u in range(width):
                advance(done + u, u % 2, False)
        done = done + (i & width)
        width //= 2

    @pl.when((i & 1) != 0)
    def _():
        advance(done, 0, False)
        advance(done + 1, 1, True)

    @pl.when((i & 1) == 0)
    def _():
        advance(done, 0, True)

    lam = _lam_value(lq1_ref, lk1_ref, lq2_ref, lk2_ref, lam_init)
    for h in range(heads):
        o = acc_scr[h, :V_DIM, :] / acc_scr[h, V_DIM:V_DIM + 1, :]
        o = (o[:, :tq] - lam * o[:, tq:]).T
        o_ref[:, h * V_DIM:(h + 1) * V_DIM] = (
            _rms_norm(o, g_ref[...]) * (1.0 - lam_init)).astype(o_ref.dtype)


def _lam_specs(layer):
    idx = lambda *_: (layer, 0, 0)
    return [pl.BlockSpec((None, 1, HEAD_DIM), idx)] * 4 + [pl.BlockSpec((None, 1, V_DIM), idx)]


def _attn_prompt(q, k, v, layer, lam_w, lam_init):
    batch, seq, _ = k.shape
    tq = ATTN_TILE
    hs = ATTN_HEADS
    q_spec = pl.BlockSpec((None, hs * 2 * HEAD_DIM, tq), lambda b, h, i: (b, h, i))
    k_spec = pl.BlockSpec((None, seq, hs * 2 * HEAD_DIM), lambda b, h, i: (b, 0, h))
    v_spec = pl.BlockSpec((None, hs * V_DIM, seq), lambda b, h, i: (b, h, 0))
    return pl.pallas_call(
        functools.partial(_attn_prompt_kernel, lam_init=lam_init),
        grid=(batch, N_DH // hs, seq // tq),
        in_specs=[q_spec, k_spec, v_spec] + _lam_specs(layer),
        out_specs=pl.BlockSpec((None, tq, hs * V_DIM), lambda b, h, i: (b, i, h)),
        out_shape=jax.ShapeDtypeStruct((batch, seq, N_DH * V_DIM), BF16),
        scratch_shapes=[pltpu.VMEM((hs, 1, 2 * tq), F32),
                        pltpu.VMEM((hs, V_DIM + BF16_SUBLANES, 2 * tq), F32),
                        pltpu.VMEM((hs, 2, tq, 2 * tq), F32)],
        compiler_params=_params(("parallel", "parallel", "arbitrary")),
    )(q, k, v, *lam_w)


def _decode_reset(m_scr, l_scr, acc_scr):
    m_scr[...] = jnp.full(m_scr.shape, MASK_VALUE, F32)
    l_scr[...] = jnp.zeros_like(l_scr)
    acc_scr[...] = jnp.zeros_like(acc_scr)


def _decode_query(q_ref):
    width = q_ref.shape[1]
    map_row = lax.broadcasted_iota(jnp.int32, (N_MAPS, width), 0)
    lane = lax.broadcasted_iota(jnp.int32, (N_MAPS, width), 1)
    qbd = jnp.where(lane // HEAD_DIM == map_row, jnp.broadcast_to(q_ref[...], (N_MAPS, width)), 0.0)
    return qbd


def _decode_scores(qbd, k_ref):
    qb = qbd.astype(BF16)
    return jnp.concatenate(
        [jnp.dot(qb, k_ref[i].astype(BF16), preferred_element_type=F32) for i in range(k_ref.shape[0])],
        axis=1)


def _decode_update(s, v_ref, m_scr, l_scr, acc_scr):
    pages = v_ref.shape[0]
    rows = PAGE_SIZE * N_DH
    m_prev = m_scr[...]
    m_next = jnp.maximum(m_prev, jnp.max(s, axis=1, keepdims=True))
    alpha = jnp.exp(m_prev - m_next)
    p = jnp.exp(s - _repeat(m_next, pages, axis=1))
    l_scr[...] = alpha * l_scr[...] + jnp.sum(p, axis=1, keepdims=True)
    p_rows = jnp.concatenate([p[:, i * PAGE_SIZE:(i + 1) * PAGE_SIZE] for i in range(pages)],
                             axis=0).astype(BF16)
    token = lax.broadcasted_iota(jnp.int32, (PAGE_SIZE, rows), 0)
    row = lax.broadcasted_iota(jnp.int32, (PAGE_SIZE, rows), 1)
    expand = jnp.where(row // N_DH == token, 1.0, 0.0).astype(BF16)
    spread = jnp.dot(p_rows, expand, preferred_element_type=F32)
    r = lax.broadcasted_iota(jnp.int32, spread.shape, 0)
    c = lax.broadcasted_iota(jnp.int32, spread.shape, 1)
    spread = jnp.where(c % N_DH == (r % N_MAPS) // 2, spread, 0.0)
    weights = jnp.concatenate([spread[i * N_MAPS:(i + 1) * N_MAPS, :] for i in range(pages)],
                              axis=1).astype(BF16)
    values = v_ref[...].reshape(pages * rows, V_DIM).astype(BF16)
    acc_scr[...] = alpha * acc_scr[...] + jnp.dot(weights, values, preferred_element_type=F32)
    m_scr[...] = m_next


def _decode_finish(qbd, kn_ref, vn_ref, lam_refs, g_ref, m_scr, l_scr, acc_scr, lam_init):
    s_new = jnp.sum(qbd * kn_ref[...], axis=1, keepdims=True)
    m_prev = m_scr[...]
    m_fin = jnp.maximum(m_prev, s_new)
    alpha = jnp.exp(m_prev - m_fin)
    p_new = jnp.exp(s_new - m_fin)
    l_fin = alpha * l_scr[...] + p_new
    vn = vn_ref[...]
    vn_rows = jnp.concatenate([vn[:, (j // 2) * V_DIM:(j // 2 + 1) * V_DIM] for j in range(N_MAPS)],
                              axis=0)
    res = (alpha * acc_scr[...] + p_new * vn_rows) / l_fin
    lam = _lam_value(*lam_refs, lam_init)
    g = g_ref[...]
    return jnp.concatenate(
        [_rms_norm(res[2 * h:2 * h + 1, :] - lam * res[2 * h + 1:2 * h + 2, :], g) for h in range(N_DH)],
        axis=1) * (1.0 - lam_init)


def _merge_math(x, rnn_out, att, gr, ga, g1, wpr_ref, wpa_ref, wout_ref):
    pr = jnp.dot(rnn_out.astype(BF16), wpr_ref[...], preferred_element_type=F32)
    pa = jnp.dot(att.astype(BF16), wpa_ref[...], preferred_element_type=F32)
    merged = _sigmoid(gr.astype(F32)) * pr + _sigmoid(ga.astype(F32)) * pa
    out = jnp.dot(merged.astype(BF16), wout_ref[...], preferred_element_type=F32)
    return x + g1 * out


def _merge_kernel(x_ref, rnn_ref, att_ref, gr_ref, ga_ref, g1_ref, wpr_ref, wpa_ref, wout_ref, o_ref):
    o_ref[...] = _merge_math(x_ref[...], rnn_ref[...], att_ref[...], gr_ref[...], ga_ref[...],
                             g1_ref[...], wpr_ref, wpa_ref, wout_ref)


def _merge(x, rnn_out, att, gr, ga, mod, layer, w_pr, w_pa, w_out, *, tm):
    groups, rows, d = x.shape
    tile = lambda a: pl.BlockSpec((None, tm, a.shape[2]), lambda g, i: (g, i, 0))
    weight = lambda w: pl.BlockSpec((None,) + w.shape[1:], lambda g, i: (layer, 0, 0))
    return pl.pallas_call(
        _merge_kernel,
        grid=(groups, rows // tm),
        in_specs=[tile(x), tile(rnn_out), tile(att), tile(gr), tile(ga),
                  _mod_spec(layer, 2, mod.shape[2], d), weight(w_pr), weight(w_pa), weight(w_out)],
        out_specs=tile(x),
        out_shape=jax.ShapeDtypeStruct(x.shape, F32),
        compiler_params=_params(("parallel", "parallel")),
    )(x, rnn_out, att, gr, ga, mod, w_pr, w_pa, w_out)


def _mlp_hidden(x, sh_ref, sc_ref, ng_ref):
    return (_rms_norm(x, ng_ref[...]) * (1.0 + sc_ref[...]) + sh_ref[...]).astype(BF16)


def _ff_up(hb, wup_ref, lo):
    u = jnp.maximum(jnp.dot(hb, wup_ref[:, lo:lo + FF_CHUNK], preferred_element_type=F32), 0.0)
    return (u * u).astype(BF16)


def _ff_down(u, wdn_ref, lo):
    return jnp.dot(u, wdn_ref[lo:lo + FF_CHUNK, :], preferred_element_type=F32)


def _ff_chunk(hb, wup_ref, wdn_ref, lo):
    return _ff_down(_ff_up(hb, wup_ref, lo), wdn_ref, lo)


def _mlp_finish(x, acc, g2_ref, fg_ref, o_ref):
    y = x + g2_ref[...] * acc
    o_ref[...] = y if fg_ref is None else _rms_norm(y, fg_ref[...])


def _mlp_kernel(x_ref, sh_ref, sc_ref, g2_ref, ng_ref, wup_ref, wdn_ref, *rest, final):
    fg_ref = rest[0] if final else None
    o_ref = rest[-1]
    x = x_ref[...]
    hb = _mlp_hidden(x, sh_ref, sc_ref, ng_ref)
    acc = jnp.zeros(x.shape, F32)
    for lo in range(0, wup_ref.shape[1], FF_CHUNK):
        acc = acc + _ff_chunk(hb, wup_ref, wdn_ref, lo)
    _mlp_finish(x, acc, g2_ref, fg_ref, o_ref)


def _mlp_decode_kernel(pt_ref, x_ref, sh_ref, sc_ref, g2_ref, ng_ref, wup_ref, wdn_ref, *rest,
                       final, lam_init, layer, n_pages):
    fg_ref = rest[0] if final else None
    (q_ref, kn_ref, vn_ref, lq1_ref, lk1_ref, lq2_ref, lk2_ref, sg_ref, ck_ref, cv_ref,
     o_ref, os_ref, kbuf, vbuf, sem, m_scr, l_scr, acc_scr) = rest[1 if final else 0:]
    steps = pl.num_programs(1)
    t = pl.program_id(0) * steps + pl.program_id(1)
    total = pl.num_programs(0) * steps
    n_chunks = n_pages // DECODE_PAGES
    n_ff = wup_ref.shape[1] // FF_CHUNK

    def copies(which, seq, chunk, slot):
        src, dst = ((ck_ref, kbuf), (cv_ref, vbuf))[which]
        return [pltpu.make_async_copy(src.at[layer, pt_ref[seq * n_pages + chunk * DECODE_PAGES + p]],
                                      dst.at[slot, p], sem.at[slot, which])
                for p in range(DECODE_PAGES)]

    def start(which, seq, chunk, slot):
        for copy in copies(which, seq, chunk, slot):
            copy.start()

    def wait(which, seq, chunk, slot):
        for copy in copies(which, seq, chunk, slot):
            copy.wait()

    def start_ahead(which, c, slot):
        if c + DECODE_SLOTS < n_chunks:
            start(which, t, c + DECODE_SLOTS, slot)
        else:
            @pl.when(t + 1 < total)
            def _():
                start(which, t + 1, c + DECODE_SLOTS - n_chunks, slot)

    @pl.when(t == 0)
    def _():
        for c in range(DECODE_SLOTS):
            start(0, 0, c, c)
            start(1, 0, c, c)

    qbd = _decode_query(q_ref)
    _decode_reset(m_scr, l_scr, acc_scr)
    x = x_ref[...]
    hb = _mlp_hidden(x, sh_ref, sc_ref, ng_ref)
    acc = jnp.zeros(x.shape, F32)
    u = None
    n_pieces = 2 * n_ff
    for c in range(n_chunks):
        slot = c % DECODE_SLOTS
        wait(0, t, c, slot)
        s = _decode_scores(qbd, kbuf.at[slot])
        start_ahead(0, c, slot)
        wait(1, t, c, slot)
        _decode_update(s, vbuf.at[slot], m_scr, l_scr, acc_scr)
        start_ahead(1, c, slot)
        for piece in range(c * n_pieces // n_chunks, (c + 1) * n_pieces // n_chunks):
            lo = (piece // 2) * FF_CHUNK
            if piece % 2 == 0:
                u = _ff_up(hb, wup_ref, lo)
            else:
                acc = acc + _ff_down(u, wdn_ref, lo)
    os_ref[...] = _decode_finish(qbd, kn_ref, vn_ref,
                                 (lq1_ref, lk1_ref, lq2_ref, lk2_ref), sg_ref,
                                 m_scr, l_scr, acc_scr, lam_init)
    _mlp_finish(x, acc, g2_ref, fg_ref, o_ref)


def _mlp(x, mod, layer, norm_g, w_up, w_down, final_g, *, tm, final):
    groups, rows, d = x.shape
    tile = pl.BlockSpec((None, tm, d), lambda g, i: (g, i, 0))
    weight = lambda w: pl.BlockSpec((None,) + w.shape[1:], lambda g, i: (layer, 0, 0))
    mod_rows = mod.shape[2]
    in_specs = [tile, _mod_spec(layer, 3, mod_rows, d), _mod_spec(layer, 4, mod_rows, d),
                _mod_spec(layer, 5, mod_rows, d), pl.BlockSpec((None, 1, d), lambda g, i: (layer, 0, 0)),
                weight(w_up), weight(w_down)]
    args = [x, mod, mod, mod, norm_g, w_up, w_down]
    if final:
        in_specs.append(pl.BlockSpec((1, d), lambda g, i: (0, 0)))
        args.append(final_g)
    return pl.pallas_call(
        functools.partial(_mlp_kernel, final=final),
        grid=(groups, rows // tm),
        in_specs=in_specs,
        out_specs=tile,
        out_shape=jax.ShapeDtypeStruct(x.shape, F32),
        compiler_params=_params(("parallel", "parallel")),
    )(*args)


def _mlp_decode(x, mod, layer, norm_g, w_up, w_down, final_g, q, k_new, v_new, cache_k, cache_v,
                page_table, lam_w, lam_init, *, tm, final):
    groups, rows, d = x.shape
    n, _, width = q.shape
    n_pages = page_table.shape[1]
    steps = rows // tm
    assert n == groups * steps, "one running sequence per MLP grid step"
    assert n_pages % (DECODE_SLOTS * DECODE_PAGES) == 0, "page chunks rotate through the slots"
    mod_rows = mod.shape[2]
    tile = pl.BlockSpec((None, tm, d), lambda g, i, pt: (g, i, 0))
    mod_spec = lambda which: pl.BlockSpec((None, None, mod_rows, d), lambda g, i, pt: (layer, g, 0, which))
    per_layer = lambda shape, **kw: pl.BlockSpec((None,) + shape, lambda g, i, pt: (layer,) + (0,) * len(shape), **kw)
    row = pl.BlockSpec((None, 1, width), lambda g, i, pt: (g * steps + i, 0, 0))
    hbm = pl.BlockSpec(memory_space=pl.ANY)
    in_specs = [tile, mod_spec(3), mod_spec(4), mod_spec(5), per_layer((1, d)),
                per_layer(w_up.shape[1:], pipeline_mode=pl.Buffered(1)),
                per_layer(w_down.shape[1:], pipeline_mode=pl.Buffered(1))]
    args = [x, mod, mod, mod, norm_g, w_up, w_down]
    if final:
        in_specs.append(pl.BlockSpec((1, d), lambda g, i, pt: (0, 0)))
        args.append(final_g)
    in_specs += [row, row, row] + [per_layer((1, HEAD_DIM))] * 4 + [per_layer((1, V_DIM)), hbm, hbm]
    args += [q, k_new, v_new, *lam_w, cache_k, cache_v]
    grid_spec = pltpu.PrefetchScalarGridSpec(
        num_scalar_prefetch=1,
        grid=(groups, steps),
        in_specs=in_specs,
        out_specs=[tile, row],
        scratch_shapes=[pltpu.VMEM((DECODE_SLOTS, DECODE_PAGES, width, PAGE_SIZE), F32),
                        pltpu.VMEM((DECODE_SLOTS, DECODE_PAGES, PAGE_SIZE * N_DH, V_DIM), F32),
                        pltpu.SemaphoreType.DMA((DECODE_SLOTS, 2)),
                        pltpu.VMEM((N_MAPS, LANES), F32), pltpu.VMEM((N_MAPS, LANES), F32),
                        pltpu.VMEM((N_MAPS, V_DIM), F32)],
    )
    return pl.pallas_call(
        functools.partial(_mlp_decode_kernel, final=final, lam_init=lam_init, layer=layer,
                          n_pages=n_pages),
        grid_spec=grid_spec,
        out_shape=[jax.ShapeDtypeStruct(x.shape, F32), jax.ShapeDtypeStruct((n, 1, width), F32)],
        compiler_params=_params(("arbitrary", "arbitrary")),
    )(page_table.reshape(-1), *args)


def kernel(x_prompt, x_sample, c_prompt, c_sample, cache_k, cache_v, state_rnn, state_conv, page_table, w_ada, b_ada, norm1_g, norm2_g, w_in, conv_w, conv_b, w_a, b_a, w_x, b_x, lru_lambda, lq1, lk1, lq2, lk2, subln_g, w_pr, w_pa, w_out, w_up, w_down, final_g):
    depth, d = norm1_g.shape
    batch, seq, _ = x_prompt.shape
    n_dec = x_sample.shape[0]
    d_att = N_DH * 2 * HEAD_DIM
    assert x_sample.shape[1] == 1, "one new token per running sequence"
    assert seq % ROW_TILE == 0 and seq % RNN_CHUNK == 0 and seq % ATTN_TILE == 0
    assert page_table.shape[1] % DECODE_PAGES == 0 and cache_k.shape[2] == PAGE_SIZE

    w_in_b, w_pr_b, w_pa_b, w_out_b, w_up_b, w_down_b, w_a_b, w_x_b = (
        w.astype(BF16) for w in (w_in, w_pr, w_pa, w_out, w_up, w_down, w_a, w_x))
    row3 = lambda a: a.reshape(depth, 1, a.shape[-1])
    rnn_w = (conv_w, row3(conv_b), w_a_b, row3(b_a), w_x_b, row3(b_x), row3(lru_lambda))
    lam_w = (row3(lq1), row3(lk1), row3(lq2), row3(lk2), row3(subln_g))
    norm1, norm2 = row3(norm1_g), row3(norm2_g)
    final_g2 = final_g.reshape(1, d)

    n_cond = n_dec + batch
    pad = -n_cond % (2 * SUBLANES)
    c_all = jnp.concatenate([c_sample, c_prompt, jnp.zeros((pad, d), F32)], axis=0)
    mods = _ada_mod(c_all, w_ada, b_ada)
    mod_s = mods[:, :n_dec].reshape(depth, 1, n_dec, 6 * d)
    mod_p = mods[:, n_dec:n_cond].reshape(depth, batch, 1, 6 * d)

    ck = jnp.transpose(cache_k, (0, 1, 3, 4, 5, 2)).reshape(depth, cache_k.shape[1], d_att, PAGE_SIZE)
    cv = cache_v.reshape(depth, cache_v.shape[1], PAGE_SIZE * N_DH, V_DIM)
    hist_t = jnp.swapaxes(state_conv, 1, 2)

    xp = x_prompt
    xs = x_sample.reshape(1, n_dec, d)
    h_p, cv_p, k_s, v_s, h_s, cv_s = ([] for _ in range(6))
    kv_p = None
    for l in range(depth):
        lam_init = 0.8 - 0.6 * math.exp(-0.3 * l)
        last = l == depth - 1

        xr, yr, qt, k_all, v_all, gr, ga, kb, vtb = _in_proj(
            xp, mod_p, l, norm1, w_in_b, tm=ROW_TILE, prompt=True, stacked=kv_p)
        kv_p = (k_all, v_all)
        att = _attn_prompt(qt, kb, vtb, l, lam_w, lam_init)
        xp, h_last, conv_last = _rnn_merge(xp, xr, yr, att, gr, ga, mod_p, l, rnn_w,
                                           w_pr_b, w_pa_b, w_out_b, tm=ROW_TILE)
        h_p.append(h_last[:, 0]); cv_p.append(conv_last)

        xr, yr, q, k, v, gr, ga = _in_proj(xs, mod_s, l, norm1, w_in_b, tm=n_dec, prompt=False)
        rnn_out, h_new, conv_new = _rnn_sample(xr[0], yr[0], hist_t, state_rnn, l, rnn_w)
        as_rows = lambda a: a.reshape(n_dec, 1, a.shape[-1])
        xp, att = _mlp_decode(xp, mod_p, l, norm2, w_up_b, w_down_b, final_g2, as_rows(q), as_rows(k),
                              as_rows(v), ck, cv, page_table, lam_w, lam_init, tm=ROW_TILE, final=last)
        xs = _merge(xs, rnn_out[None], att.reshape(1, n_dec, -1), gr, ga, mod_s, l,
                    w_pr_b, w_pa_b, w_out_b, tm=n_dec)
        xs = _mlp(xs, mod_s, l, norm2, w_up_b, w_down_b, final_g2, tm=n_dec, final=last)
        k_s.append(k[0]); v_s.append(v[0]); h_s.append(h_new); cv_s.append(jnp.swapaxes(conv_new, 0, 1))

    kshape = lambda n, t: (depth, n, t, N_DH, 2, HEAD_DIM)
    vshape = lambda n, t: (depth, n, t, N_DH, V_DIM)
    k_prompt = jnp.transpose(kv_p[0].reshape(depth, batch, N_DH, 2, HEAD_DIM, seq), (0, 1, 5, 2, 3, 4))
    return (xp, xs.reshape(n_dec, 1, d),
            k_prompt, kv_p[1].reshape(vshape(batch, seq)),
            jnp.stack(h_p), jnp.stack(cv_p),
            jnp.stack(k_s).reshape(kshape(n_dec, 1)), jnp.stack(v_s).reshape(vshape(n_dec, 1)),
            jnp.stack(h_s), jnp.stack(cv_s))
```
